```python
import jax, jax.numpy as jnp
from jax import lax
import numpy as np

D_MODEL = 1024
BATCH = 8
SEQ = 4096
DEPTH = 4

N_MIXERS = 2
EPS = 1e-6

GLA_HEADS = 4
GLA_DK = D_MODEL // 2
GLA_DV = D_MODEL
GLA_HEAD_DK = GLA_DK // GLA_HEADS
GLA_HEAD_DV = GLA_DV // GLA_HEADS
GLA_GATE_RANK = 16
GLA_GATE_TAU = 16.0
GLA_CHUNK = 64
GLA_IN = 2 * GLA_DK + 2 * GLA_DV + GLA_GATE_RANK

CONV_E = D_MODEL
CONV_K = 31
CONV_IN = 3 * CONV_E

N_GLA_LAYERS = (DEPTH + 1) // 2
N_CONV_LAYERS = DEPTH // 2

kernel_name = 'hybrid_gla_conformer_gated'


def rmsnorm(x, w):
    x32 = x.astype(jnp.float32)
    y = x32 * lax.rsqrt(jnp.mean(x32 * x32, axis=-1, keepdims=True) + EPS)
    return (y * w.astype(jnp.float32)).astype(x.dtype)


def layernorm(x, w, b):
    x32 = x.astype(jnp.float32)
    mu = jnp.mean(x32, axis=-1, keepdims=True)
    xc = x32 - mu
    var = jnp.mean(xc * xc, axis=-1, keepdims=True)
    y = xc * lax.rsqrt(var + EPS)
    return (y * w.astype(jnp.float32) + b.astype(jnp.float32)).astype(x.dtype)


def gla_chunked(q, k, v, log_a):
    C = q.shape[3]
    b = jnp.cumsum(log_a, axis=3)
    b_last = b[:, :, :, -1:, :]
    q_dec = q * jnp.exp(b)
    k_inv = k * jnp.exp(-b)
    k_end = k * jnp.exp(b_last - b)
    causal = jnp.tril(jnp.ones((C, C), dtype=bool))
    scores = jnp.einsum('bhnid,bhnjd->bhnij', q_dec, k_inv)
    scores = jnp.where(causal, scores, 0.0)
    o_intra = jnp.einsum('bhnij,bhnjv->bhniv', scores, v)

    decay = jnp.exp(b_last[:, :, :, 0, :])
    xs = (jnp.moveaxis(q_dec, 2, 0), jnp.moveaxis(k_end, 2, 0),
          jnp.moveaxis(v, 2, 0), jnp.moveaxis(decay, 2, 0))
    B, H = q.shape[0], q.shape[1]
    s0 = jnp.zeros((B, H, q.shape[-1], v.shape[-1]), jnp.float32)

    def step(S, inp):
        qd, ke, vc, dc = inp
        o_inter = jnp.einsum('bhid,bhdv->bhiv', qd, S)
        S_new = dc[..., None] * S + jnp.einsum('bhjd,bhjv->bhdv', ke, vc)
        return S_new, o_inter

    _, o_inter = lax.scan(step, s0, xs)
    return o_intra + jnp.moveaxis(o_inter, 0, 2)


def gla_mixer(h, w_in, w_g2, b_g, gn_w, w_out):
    B, T, _ = h.shape
    N = T // GLA_CHUNK
    proj = h @ w_in
    q, k, v, z, g_lr = jnp.split(
        proj, [GLA_DK, 2 * GLA_DK, 2 * GLA_DK + GLA_DV, 2 * GLA_DK + 2 * GLA_DV], axis=-1)
    log_a = jax.nn.log_sigmoid((g_lr @ w_g2 + b_g).astype(jnp.float32)) / GLA_GATE_TAU

    def heads(t, d):
        return t.astype(jnp.float32).reshape(B, N, GLA_CHUNK, GLA_HEADS, d).transpose(0, 3, 1, 2, 4)

    qh = heads(q, GLA_HEAD_DK) * (GLA_HEAD_DK ** -0.5)
    kh = heads(k, GLA_HEAD_DK)
    vh = heads(v, GLA_HEAD_DV)
    ah = heads(log_a, GLA_HEAD_DK)
    o = gla_chunked(qh, kh, vh, ah)
    o = o.transpose(0, 2, 3, 1, 4).reshape(B, T, GLA_HEADS, GLA_HEAD_DV)
    o = o * lax.rsqrt(jnp.mean(o * o, axis=-1, keepdims=True) + EPS)
    o = o * gn_w.astype(jnp.float32).reshape(GLA_HEADS, GLA_HEAD_DV)
    o = o.reshape(B, T, GLA_DV).astype(h.dtype) * jax.nn.silu(z)
    return o @ w_out


def conv_mixer(h, w_in, dw_w, dw_b, ln_w, ln_b, w_out):
    a, gl, z = jnp.split(h @ w_in, 3, axis=-1)
    u = a * jax.nn.sigmoid(gl)
    u = lax.conv_general_dilated(
        u, dw_w.reshape(CONV_K, 1, CONV_E).astype(u.dtype),
        window_strides=(1,), padding=[(CONV_K - 1, 0)],
        dimension_numbers=('NWC', 'WIO', 'NWC'),
        feature_group_count=CONV_E) + dw_b
    u = jax.nn.silu(layernorm(u, ln_w, ln_b))
    return (u * jax.nn.silu(z)) @ w_out


def setup_inputs(seed: int = 0) -> dict:
    key = jax.random.key(seed)
    ks = jax.random.split(key, 16)
    f32 = jnp.float32
    nrm = lambda k, shape, s: jax.random.normal(k, shape, f32) * s
    return {
        'x': nrm(ks[0], (BATCH, SEQ, D_MODEL), 1.0),
        'norm_w': 1.0 + nrm(ks[1], (DEPTH, D_MODEL), 0.02),
        'final_norm_w': 1.0 + nrm(ks[2], (D_MODEL,), 0.02),
        'gla_w_in': nrm(ks[3], (N_GLA_LAYERS, D_MODEL, GLA_IN), D_MODEL ** -0.5),
        'gla_w_g2': nrm(ks[4], (N_GLA_LAYERS, GLA_GATE_RANK, GLA_DK), GLA_GATE_RANK ** -0.5),
        'gla_b_g': nrm(ks[5], (N_GLA_LAYERS, GLA_DK), 0.02),
        'gla_gn_w': 1.0 + nrm(ks[6], (N_GLA_LAYERS, GLA_DV), 0.02),
        'gla_w_out': nrm(ks[7], (N_GLA_LAYERS, GLA_DV, D_MODEL), GLA_DV ** -0.5),
        'conv_w_in': nrm(ks[8], (N_CONV_LAYERS, D_MODEL, CONV_IN), D_MODEL ** -0.5),
        'conv_dw_w': nrm(ks[9], (N_CONV_LAYERS, CONV_K, CONV_E), CONV_K ** -0.5),
        'conv_dw_b': nrm(ks[10], (N_CONV_LAYERS, CONV_E), 0.02),
        'conv_ln_w': 1.0 + nrm(ks[11], (N_CONV_LAYERS, CONV_E), 0.02),
        'conv_ln_b': nrm(ks[12], (N_CONV_LAYERS, CONV_E), 0.02),
        'conv_w_out': nrm(ks[13], (N_CONV_LAYERS, CONV_E, D_MODEL), CONV_E ** -0.5),
    }


def reference(x, norm_w, final_norm_w, gla_w_in, gla_w_g2, gla_b_g, gla_gn_w, gla_w_out,
              conv_w_in, conv_dw_w, conv_dw_b, conv_ln_w, conv_ln_b, conv_w_out):
    for i in range(DEPTH):
        h = rmsnorm(x, norm_w[i])
        j = i // N_MIXERS
        if i % N_MIXERS == 0:
            y = gla_mixer(h, gla_w_in[j], gla_w_g2[j], gla_b_g[j], gla_gn_w[j], gla_w_out[j])
        else:
            y = conv_mixer(h, conv_w_in[j], conv_dw_w[j], conv_dw_b[j],
                           conv_ln_w[j], conv_ln_b[j], conv_w_out[j])
        x = x + y
    return rmsnorm(x, final_norm_w)
```

```python
import functools

import jax
import jax.numpy as jnp
from jax import lax
from jax.experimental import pallas as pl
from jax.experimental.pallas import tpu as pltpu

EPS = 1e-6

D_MODEL = 1024
GLA_HEADS = 4
GLA_DK = D_MODEL // 2
GLA_DV = D_MODEL
HEAD_DK = GLA_DK // GLA_HEADS
HEAD_DV = GLA_DV // GLA_HEADS
GATE_RANK = 16
GATE_TAU = 16.0
CHUNK = 64
CONV_K = 31

LANES = 128
GATE_PAD = LANES
GLA_IN_PAD = 2 * GLA_DK + 2 * GLA_DV + GATE_PAD
HALO = 32

TIME_BLOCK = 256
VMEM_LIMIT_BYTES = 56 * 1024 * 1024


def _sigmoid(x):
    return 1.0 / (1.0 + jnp.exp(-x))


def _rmsnorm(x, w):
    return x * lax.rsqrt(jnp.mean(x * x, axis=-1, keepdims=True) + EPS) * w


def _dot(a, b):
    return jnp.dot(a, b, preferred_element_type=jnp.float32)


def _dot_nt(a, b):
    return lax.dot_general(a, b, (((1,), (1,)), ((), ())),
                           preferred_element_type=jnp.float32)


def _dot_tn(a, b):
    return lax.dot_general(a, b, (((0,), (0,)), ((), ())),
                           preferred_element_type=jnp.float32)


def _gla_layer_kernel(x_ref, nw_ref, win_ref, wg2_ref, bg_ref, gnw_ref, wout_ref,
                      o_ref, st_ref, gated_ref):
    tb = x_ref.shape[1]
    bf16 = jnp.bfloat16

    @pl.when(pl.program_id(1) == 0)
    def _():
        st_ref[...] = jnp.zeros_like(st_ref)

    x = x_ref[0]
    h = _rmsnorm(x, nw_ref[...]).astype(bf16)
    proj = _dot(h, win_ref[...])

    o_q, o_k, o_v, o_z, o_g = 0, GLA_DK, 2 * GLA_DK, 2 * GLA_DK + GLA_DV, 2 * GLA_DK + 2 * GLA_DV
    g = _dot(proj[:, o_g:o_g + GATE_PAD].astype(bf16), wg2_ref[...]) + bg_ref[...]
    log_a = (jnp.minimum(g, 0.0) - jnp.log(1.0 + jnp.exp(-jnp.abs(g)))) * (1.0 / GATE_TAU)

    ri = lax.broadcasted_iota(jnp.int32, (tb, tb), 0)
    ci = lax.broadcasted_iota(jnp.int32, (tb, tb), 1)
    cum_mat = jnp.where((ri // CHUNK == ci // CHUNK) & (ci <= ri), 1.0, 0.0).astype(bf16)
    la_hi = log_a.astype(bf16)
    la_lo = (log_a - la_hi.astype(jnp.float32)).astype(bf16)
    b = _dot(cum_mat, la_hi) + _dot(cum_mat, la_lo)

    q = proj[:, o_q:o_q + GLA_DK] * (HEAD_DK ** -0.5)
    k = proj[:, o_k:o_k + GLA_DK]
    q_dec = (q * jnp.exp(b)).astype(bf16)
    k_inv = (k * jnp.exp(-b)).astype(bf16)
    v = proj[:, o_v:o_v + GLA_DV].astype(bf16)

    ii = lax.broadcasted_iota(jnp.int32, (CHUNK, CHUNK), 0)
    jj = lax.broadcasted_iota(jnp.int32, (CHUNK, CHUNK), 1)
    causal = jj <= ii

    for hd in range(GLA_HEADS):
        ks = slice(hd * HEAD_DK, (hd + 1) * HEAD_DK)
        vs = slice(hd * HEAD_DV, (hd + 1) * HEAD_DV)
        st = st_ref[hd]
        gn_w = gnw_ref[:, vs]
        for c in range(tb // CHUNK):
            rs = slice(c * CHUNK, (c + 1) * CHUNK)
            b_c = b[rs, ks]
            b_last = b_c[CHUNK - 1:CHUNK, :]
            k_end = (k[rs, ks] * jnp.exp(b_last - b_c)).astype(bf16)
            qd = q_dec[rs, ks]
            v_c = v[rs, vs]
            s = _dot_nt(qd, k_inv[rs, ks])
            s = jnp.where(causal, s, 0.0).astype(bf16)
            o = _dot(s, v_c) + _dot_nt(qd, st.astype(bf16))
            st = st * jnp.exp(b_last) + _dot_tn(v_c, k_end)
            o = o * lax.rsqrt(jnp.mean(o * o, axis=-1, keepdims=True) + EPS) * gn_w
            z = proj[rs, o_z + hd * HEAD_DV:o_z + (hd + 1) * HEAD_DV]
            gated_ref[rs, vs] = (o * (z * _sigmoid(z))).astype(bf16)
        st_ref[hd] = st

    y = _dot(gated_ref[...], wout_ref[...])
    o_ref[0] = x + y


def _conv_layer_kernel(x_ref, nw_ref, win_ref, dww_ref, dwb_ref, lnw_ref, lnb_ref, wout_ref,
                       fnw_ref, o_ref, ubuf_ref, *, final_norm):
    tb = x_ref.shape[1]
    bf16 = jnp.bfloat16
    e = D_MODEL

    @pl.when(pl.program_id(1) == 0)
    def _():
        ubuf_ref[0:HALO, :] = jnp.zeros((HALO, e), jnp.float32)

    x = x_ref[0]
    h = _rmsnorm(x, nw_ref[...]).astype(bf16)
    proj = _dot(h, win_ref[...])
    u = proj[:, 0:e] * _sigmoid(proj[:, e:2 * e])
    ubuf_ref[HALO:HALO + tb, :] = u

    off = HALO - (CONV_K - 1)
    acc = jnp.zeros((tb, e), jnp.float32) + dwb_ref[...]
    for kk in range(CONV_K):
        acc = acc + ubuf_ref[off + kk:off + kk + tb, :] * dww_ref[kk:kk + 1, :]
    ubuf_ref[0:HALO, :] = ubuf_ref[tb:tb + HALO, :]

    mu = jnp.mean(acc, axis=-1, keepdims=True)
    xc = acc - mu
    var = jnp.mean(xc * xc, axis=-1, keepdims=True)
    ln = xc * lax.rsqrt(var + EPS) * lnw_ref[...] + lnb_ref[...]
    z = proj[:, 2 * e:3 * e]
    gated = ((ln * _sigmoid(ln)) * (z * _sigmoid(z))).astype(bf16)
    y = x + _dot(gated, wout_ref[...])
    if final_norm:
        y = _rmsnorm(y, fnw_ref[...])
    o_ref[0] = y


def _const_spec(shape):
    return pl.BlockSpec(shape, lambda b, t: (0,) * len(shape))


def _compiler_params():
    return pltpu.CompilerParams(dimension_semantics=("arbitrary", "arbitrary"),
                                vmem_limit_bytes=VMEM_LIMIT_BYTES)


def _gla_layer(x, norm_w, w_in, w_g2, b_g, gn_w, w_out):
    bsz, seq, d = x.shape
    tb = TIME_BLOCK
    x_spec = pl.BlockSpec((1, tb, d), lambda b, t: (b, t, 0))
    return pl.pallas_call(
        _gla_layer_kernel,
        grid=(bsz, seq // tb),
        in_specs=[x_spec, _const_spec(norm_w.shape), _const_spec(w_in.shape),
                  _const_spec(w_g2.shape), _const_spec(b_g.shape), _const_spec(gn_w.shape),
                  _const_spec(w_out.shape)],
        out_specs=x_spec,
        out_shape=jax.ShapeDtypeStruct(x.shape, x.dtype),
        scratch_shapes=[pltpu.VMEM((GLA_HEADS, HEAD_DV, HEAD_DK), jnp.float32),
                        pltpu.VMEM((tb, GLA_DV), jnp.bfloat16)],
        compiler_params=_compiler_params(),
        name="gla_layer",
    )(x, norm_w, w_in, w_g2, b_g, gn_w, w_out)


def _conv_layer(x, norm_w, w_in, dw_w, dw_b, ln_w, ln_b, w_out, final_w, final_norm):
    bsz, seq, d = x.shape
    tb = TIME_BLOCK
    x_spec = pl.BlockSpec((1, tb, d), lambda b, t: (b, t, 0))
    return pl.pallas_call(
        functools.partial(_conv_layer_kernel, final_norm=final_norm),
        grid=(bsz, seq // tb),
        in_specs=[x_spec, _const_spec(norm_w.shape), _const_spec(w_in.shape),
                  _const_spec(dw_w.shape), _const_spec(dw_b.shape), _const_spec(ln_w.shape),
                  _const_spec(ln_b.shape), _const_spec(w_out.shape), _const_spec(final_w.shape)],
        out_specs=x_spec,
        out_shape=jax.ShapeDtypeStruct(x.shape, x.dtype),
        scratch_shapes=[pltpu.VMEM((HALO + tb, d), jnp.float32)],
        compiler_params=_compiler_params(),
        name="conv_layer_final" if final_norm else "conv_layer",
    )(x, norm_w, w_in, dw_w, dw_b, ln_w, ln_b, w_out, final_w)


def kernel(x, norm_w, final_norm_w, gla_w_in, gla_w_g2, gla_b_g, gla_gn_w, gla_w_out,
           conv_w_in, conv_dw_w, conv_dw_b, conv_ln_w, conv_ln_b, conv_w_out):
    bf16 = jnp.bfloat16
    depth = norm_w.shape[0]
    row = lambda a: a.reshape(1, -1)
    for i in range(depth):
        j = i // 2
        nw = row(norm_w[i])
        if i % 2 == 0:
            w_in = jnp.pad(gla_w_in[j], ((0, 0), (0, GATE_PAD - GATE_RANK))).astype(bf16)
            w_g2 = jnp.pad(gla_w_g2[j], ((0, GATE_PAD - GATE_RANK), (0, 0))).astype(bf16)
            x = _gla_layer(x, nw, w_in, w_g2, row(gla_b_g[j]), row(gla_gn_w[j]),
                           gla_w_out[j].astype(bf16))
        else:
            dw_w = jnp.pad(conv_dw_w[j], ((0, HALO - CONV_K), (0, 0)))
            x = _conv_layer(x, nw, conv_w_in[j].astype(bf16), dw_w, row(conv_dw_b[j]),
                            row(conv_ln_w[j]), row(conv_ln_b[j]), conv_w_out[j].astype(bf16),
                            row(final_norm_w), final_norm=(i == depth - 1))
    return x
```

```python
import functools

import jax
import jax.numpy as jnp
from jax import lax
from jax.experimental import pallas as pl
from jax.experimental.pallas import tpu as pltpu

EPS = 1e-6

D_MODEL = 1024
GLA_HEADS = 4
GLA_DK = D_MODEL // 2
GLA_DV = D_MODEL
HEAD_DK = GLA_DK // GLA_HEADS
HEAD_DV = GLA_DV // GLA_HEADS
GATE_RANK = 16
GATE_TAU = 16.0
CHUNK = 64
CONV_K = 31

LANES = 128
SUBLANES = 8
GATE_PAD = LANES
GLA_IN_PAD = 2 * GLA_DK + 2 * GLA_DV + GATE_PAD
HALO = 32

TIME_BLOCK = 256
VMEM_LIMIT_BYTES = 56 * 1024 * 1024


def _sigmoid(x):
    return 1.0 / (1.0 + jnp.exp(-x))


def _rmsnorm(x, w):
    return x * lax.rsqrt(jnp.mean(x * x, axis=-1, keepdims=True) + EPS) * w


def _dot(a, b):
    return jnp.dot(a, b, preferred_element_type=jnp.float32)


def _dot_nt(a, b):
    return lax.dot_general(a, b, (((1,), (1,)), ((), ())),
                           preferred_element_type=jnp.float32)


def _dot_tn(a, b):
    return lax.dot_general(a, b, (((0,), (0,)), ((), ())),
                           preferred_element_type=jnp.float32)


def _gla_layer_kernel(x_ref, nw_ref, win_ref, wg2_ref, bg_ref, gnw_ref, wout_ref,
                      o_ref, st_ref, gated_ref):
    tb = x_ref.shape[1]
    bf16 = jnp.bfloat16

    @pl.when(pl.program_id(1) == 0)
    def _():
        st_ref[...] = jnp.zeros_like(st_ref)

    x = x_ref[0]
    h = _rmsnorm(x, nw_ref[...]).astype(bf16)
    proj = _dot(h, win_ref[...])

    o_q, o_k, o_v, o_z, o_g = 0, GLA_DK, 2 * GLA_DK, 2 * GLA_DK + GLA_DV, 2 * GLA_DK + 2 * GLA_DV
    g = _dot(proj[:, o_g:o_g + GATE_PAD].astype(bf16), wg2_ref[...]) + bg_ref[...]
    log_a = (jnp.minimum(g, 0.0) - jnp.log(1.0 + jnp.exp(-jnp.abs(g)))) * (1.0 / GATE_TAU)

    ri = lax.broadcasted_iota(jnp.int32, (tb, tb), 0)
    ci = lax.broadcasted_iota(jnp.int32, (tb, tb), 1)
    cum_mat = jnp.where((ri // CHUNK == ci // CHUNK) & (ci <= ri), 1.0, 0.0).astype(bf16)
    la_hi = log_a.astype(bf16)
    la_lo = (log_a - la_hi.astype(jnp.float32)).astype(bf16)
    b = _dot(cum_mat, la_hi) + _dot(cum_mat, la_lo)

    q = proj[:, o_q:o_q + GLA_DK] * (HEAD_DK ** -0.5)
    k = proj[:, o_k:o_k + GLA_DK]
    q_dec = (q * jnp.exp(b)).astype(bf16)
    k_inv = (k * jnp.exp(-b)).astype(bf16)
    v = proj[:, o_v:o_v + GLA_DV].astype(bf16)

    ii = lax.broadcasted_iota(jnp.int32, (CHUNK, CHUNK), 0)
    jj = lax.broadcasted_iota(jnp.int32, (CHUNK, CHUNK), 1)
    causal = jj <= ii

    for hd in range(GLA_HEADS):
        ks = slice(hd * HEAD_DK, (hd + 1) * HEAD_DK)
        vs = slice(hd * HEAD_DV, (hd + 1) * HEAD_DV)
        st = st_ref[hd]
        gn_w = gnw_ref[:, vs]
        for c in range(tb // CHUNK):
            rs = slice(c * CHUNK, (c + 1) * CHUNK)
            b_c = b[rs, ks]
            b_last = b_c[CHUNK - 1:CHUNK, :]
            k_end = (k[rs, ks] * jnp.exp(b_last - b_c)).astype(bf16)
            qd = q_dec[rs, ks]
            v_c = v[rs, vs]
            s = _dot_nt(qd, k_inv[rs, ks])
            s = jnp.where(causal, s, 0.0).astype(bf16)
            o = _dot(s, v_c) + _dot_nt(qd, st.astype(bf16))
            st = st * jnp.exp(b_last) + _dot_tn(v_c, k_end)
            o = o * lax.rsqrt(jnp.mean(o * o, axis=-1, keepdims=True) + EPS) * gn_w
            z = proj[rs, o_z + hd * HEAD_DV:o_z + (hd + 1) * HEAD_DV]
            gated_ref[rs, vs] = (o * (z * _sigmoid(z))).astype(bf16)
        st_ref[hd] = st

    y = _dot(gated_ref[...], wout_ref[...])
    o_ref[0] = x + y


def _conv_layer_kernel(x_ref, nw_ref, win_ref, dww_ref, dwb_ref, lnw_ref, lnb_ref, wout_ref,
                       fnw_ref, o_ref, ubuf_ref, *, final_norm):
    tb = x_ref.shape[1]
    bf16 = jnp.bfloat16
    e = D_MODEL

    ng = e // LANES

    @pl.when(pl.program_id(1) == 0)
    def _():
        ubuf_ref[:, 0:HALO, :] = jnp.zeros((ng, HALO, LANES), jnp.float32)

    x = x_ref[0]
    h = _rmsnorm(x, nw_ref[...]).astype(bf16)
    proj = _dot(h, win_ref[...])
    u = proj[:, 0:e] * _sigmoid(proj[:, e:2 * e])
    off = HALO - (CONV_K - 1)
    cols = []
    for l in range(ng):
        ls = slice(l * LANES, (l + 1) * LANES)
        ubuf_ref[l, HALO:HALO + tb, :] = u[:, ls]
        a = dwb_ref[:, ls]
        for kk in range(CONV_K):
            a = a + ubuf_ref[l, off + kk:off + kk + tb, :] * dww_ref[kk:kk + 1, ls]
        cols.append(a)
        ubuf_ref[l, 0:HALO, :] = ubuf_ref[l, tb:tb + HALO, :]
    acc = jnp.concatenate(cols, axis=-1)

    mu = jnp.mean(acc, axis=-1, keepdims=True)
    xc = acc - mu
    var = jnp.mean(xc * xc, axis=-1, keepdims=True)
    ln = xc * lax.rsqrt(var + EPS) * lnw_ref[...] + lnb_ref[...]
    z = proj[:, 2 * e:3 * e]
    gated = ((ln * _sigmoid(ln)) * (z * _sigmoid(z))).astype(bf16)
    y = x + _dot(gated, wout_ref[...])
    if final_norm:
        y = _rmsnorm(y, fnw_ref[...])
    o_ref[0] = y


def _const_spec(shape):
    return pl.BlockSpec(shape, lambda b, t: (0,) * len(shape))


def _compiler_params():
    return pltpu.CompilerParams(dimension_semantics=("arbitrary", "arbitrary"),
                                vmem_limit_bytes=VMEM_LIMIT_BYTES)


def _gla_layer(x, norm_w, w_in, w_g2, b_g, gn_w, w_out):
    bsz, seq, d = x.shape
    tb = TIME_BLOCK
    x_spec = pl.BlockSpec((1, tb, d), lambda b, t: (b, t, 0))
    return pl.pallas_call(
        _gla_layer_kernel,
        grid=(bsz, seq // tb),
        in_specs=[x_spec, _const_spec(norm_w.shape), _const_spec(w_in.shape),
                  _const_spec(w_g2.shape), _const_spec(b_g.shape), _const_spec(gn_w.shape),
                  _const_spec(w_out.shape)],
        out_specs=x_spec,
        out_shape=jax.ShapeDtypeStruct(x.shape, x.dtype),
        scratch_shapes=[pltpu.VMEM((GLA_HEADS, HEAD_DV, HEAD_DK), jnp.float32),
                        pltpu.VMEM((tb, GLA_DV), jnp.bfloat16)],
        compiler_params=_compiler_params(),
        name="gla_layer",
    )(x, norm_w, w_in, w_g2, b_g, gn_w, w_out)


def _conv_layer(x, norm_w, w_in, dw_w, dw_b, ln_w, ln_b, w_out, final_w, final_norm):
    bsz, seq, d = x.shape
    tb = TIME_BLOCK
    x_spec = pl.BlockSpec((1, tb, d), lambda b, t: (b, t, 0))
    return pl.pallas_call(
        functools.partial(_conv_layer_kernel, final_norm=final_norm),
        grid=(bsz, seq // tb),
        in_specs=[x_spec, _const_spec(norm_w.shape), _const_spec(w_in.shape),
                  _const_spec(dw_w.shape), _const_spec(dw_b.shape), _const_spec(ln_w.shape),
                  _const_spec(ln_b.shape), _const_spec(w_out.shape), _const_spec(final_w.shape)],
        out_specs=x_spec,
        out_shape=jax.ShapeDtypeStruct(x.shape, x.dtype),
        scratch_shapes=[pltpu.VMEM((d // LANES, HALO + tb, LANES), jnp.float32)],
        compiler_params=_compiler_params(),
        name="conv_layer_final" if final_norm else "conv_layer",
    )(x, norm_w, w_in, dw_w, dw_b, ln_w, ln_b, w_out, final_w)


def kernel(x, norm_w, final_norm_w, gla_w_in, gla_w_g2, gla_b_g, gla_gn_w, gla_w_out,
           conv_w_in, conv_dw_w, conv_dw_b, conv_ln_w, conv_ln_b, conv_w_out):
    bf16 = jnp.bfloat16
    depth = norm_w.shape[0]
    row = lambda a: a.reshape(1, -1)
    for i in range(depth):
        j = i // 2
        nw = row(norm_w[i])
        if i % 2 == 0:
            w_in = jnp.pad(gla_w_in[j], ((0, 0), (0, GATE_PAD - GATE_RANK))).astype(bf16)
            w_g2 = jnp.pad(gla_w_g2[j], ((0, GATE_PAD - GATE_RANK), (0, 0))).astype(bf16)
            x = _gla_layer(x, nw, w_in, w_g2, row(gla_b_g[j]), row(gla_gn_w[j]),
                           gla_w_out[j].astype(bf16))
        else:
            dw_w = jnp.pad(conv_dw_w[j], ((0, HALO - CONV_K), (0, 0)))
            x = _conv_layer(x, nw, conv_w_in[j].astype(bf16), dw_w, row(conv_dw_b[j]),
                            row(conv_ln_w[j]), row(conv_ln_b[j]), conv_w_out[j].astype(bf16),
                            row(final_norm_w), final_norm=(i == depth - 1))
    return x
```

```python
import functools

import jax
import jax.numpy as jnp
from jax import lax
from jax.experimental import pallas as pl
from jax.experimental.pallas import tpu as pltpu

EPS = 1e-6

D_MODEL = 1024
GLA_HEADS = 4
GLA_DK = D_MODEL // 2
GLA_DV = D_MODEL
HEAD_DK = GLA_DK // GLA_HEADS
HEAD_DV = GLA_DV // GLA_HEADS
GATE_RANK = 16
GATE_TAU = 16.0
CHUNK = 64
CONV_K = 31

LANES = 128
GATE_PAD = LANES
HALO = 32

TIME_BLOCK = 256
GLA_SUBBLOCKS = 2
VMEM_LIMIT_BYTES = 56 * 1024 * 1024


NEG_LOG2_E = -1.4426950408889634


def _sigmoid(x):
    return 1.0 / (1.0 + jnp.exp2(x * NEG_LOG2_E))


def _rmsnorm(x, w):
    return x * lax.rsqrt(jnp.mean(x * x, axis=-1, keepdims=True) + EPS) * w


def _dot(a, b):
    return jnp.dot(a, b, preferred_element_type=jnp.float32)


def _dot_nt(a, b):
    return lax.dot_general(a, b, (((1,), (1,)), ((), ())),
                           preferred_element_type=jnp.float32)


def _dot_tn(a, b):
    return lax.dot_general(a, b, (((0,), (0,)), ((), ())),
                           preferred_element_type=jnp.float32)


def _gla_layer_kernel(x_ref, nw_ref, win_ref, wg2_ref, bg_ref, gnw_ref, wout_ref,
                      o_ref, st_ref, gated_ref):
    @pl.when(pl.program_id(1) == 0)
    def _():
        st_ref[...] = jnp.zeros_like(st_ref)

    tb = TIME_BLOCK
    for sb in range(x_ref.shape[1] // tb):
        rows = slice(sb * tb, (sb + 1) * tb)
        o_ref[0, rows, :] = _gla_block(x_ref[0, rows, :], nw_ref, win_ref, wg2_ref, bg_ref, gnw_ref,
                                       wout_ref, st_ref, gated_ref.at[rows])


def _gla_block(x, nw_ref, win_ref, wg2_ref, bg_ref, gnw_ref, wout_ref, st_ref, gated_ref):
    tb = x.shape[0]
    bf16 = jnp.bfloat16
    h = _rmsnorm(x, nw_ref[...]).astype(bf16)
    proj = _dot(h, win_ref[...])

    o_q, o_k, o_v, o_z, o_g = 0, GLA_DK, 2 * GLA_DK, 2 * GLA_DK + GLA_DV, 2 * GLA_DK + 2 * GLA_DV
    g = _dot(proj[:, o_g:o_g + GATE_PAD].astype(bf16), wg2_ref[...]) + bg_ref[...]
    log_a = (jnp.minimum(g, 0.0) - jnp.log(1.0 + jnp.exp(-jnp.abs(g)))) * (1.0 / GATE_TAU)

    ri = lax.broadcasted_iota(jnp.int32, (tb, tb), 0)
    ci = lax.broadcasted_iota(jnp.int32, (tb, tb), 1)
    in_chunk_causal = (ri // CHUNK == ci // CHUNK) & (ci <= ri)
    cum_mat = jnp.where(in_chunk_causal, 1.0, 0.0).astype(bf16)
    la_hi = log_a.astype(bf16)
    la_lo = (log_a - la_hi.astype(jnp.float32)).astype(bf16)
    b = _dot(cum_mat, la_hi) + _dot(cum_mat, la_lo)

    q = proj[:, o_q:o_q + GLA_DK] * (HEAD_DK ** -0.5)
    k = proj[:, o_k:o_k + GLA_DK]
    nchunk = tb // CHUNK
    b_last = [b[(c + 1) * CHUNK - 1:(c + 1) * CHUNK, :] for c in range(nchunk)]
    b_end = jnp.concatenate([jnp.broadcast_to(bl, (CHUNK, GLA_DK)) for bl in b_last], axis=0)
    q_dec = (q * jnp.exp(b)).astype(bf16)
    k_inv = (k * jnp.exp(-b)).astype(bf16)
    k_end = (k * jnp.exp(b_end - b)).astype(bf16)
    decay = [jnp.exp(bl) for bl in b_last]
    v = proj[:, o_v:o_v + GLA_DV].astype(bf16)
    zeros_k = jnp.zeros((CHUNK, HEAD_DK), bf16)

    for hd in range(GLA_HEADS):
        ks = slice(hd * HEAD_DK, (hd + 1) * HEAD_DK)
        vs = slice(hd * HEAD_DV, (hd + 1) * HEAD_DV)
        qd, v_h, ke = q_dec[:, ks], v[:, vs], k_end[:, ks]
        s = _dot_nt(qd, k_inv[:, ks])
        o = _dot(jnp.where(in_chunk_causal, s, 0.0).astype(bf16), v_h)
        incr = []
        for p in range(nchunk // 2):
            r0 = 2 * p * CHUNK
            ke_pair = jnp.concatenate(
                [jnp.concatenate([ke[r0:r0 + CHUNK], zeros_k], axis=1),
                 jnp.concatenate([zeros_k, ke[r0 + CHUNK:r0 + 2 * CHUNK]], axis=1)], axis=0)
            both = _dot_tn(v_h[r0:r0 + 2 * CHUNK], ke_pair)
            incr += [both[:, 0:HEAD_DK], both[:, HEAD_DK:2 * HEAD_DK]]
        st = st_ref[hd]
        inter = []
        for c in range(nchunk):
            inter.append(_dot_nt(qd[c * CHUNK:(c + 1) * CHUNK], st.astype(bf16)))
            st = st * decay[c][:, ks] + incr[c]
        st_ref[hd] = st
        o = o + jnp.concatenate(inter, axis=0)
        o = o * lax.rsqrt(jnp.mean(o * o, axis=-1, keepdims=True) + EPS) * gnw_ref[:, vs]
        z = proj[:, o_z + hd * HEAD_DV:o_z + (hd + 1) * HEAD_DV]
        gated_ref[:, vs] = (o * (z * _sigmoid(z))).astype(bf16)

    return x + _dot(gated_ref[...], wout_ref[...])


def _conv_layer_kernel(x_ref, nw_ref, win_ref, dww_ref, dwb_ref, lnw_ref, lnb_ref, wout_ref,
                       fnw_ref, o_ref, ubuf_ref, *, final_norm):
    tb = x_ref.shape[1]
    bf16 = jnp.bfloat16
    e = D_MODEL
    ng = e // LANES

    @pl.when(pl.program_id(1) == 0)
    def _():
        ubuf_ref[:, 0:HALO, :] = jnp.zeros((ng, HALO, LANES), jnp.float32)

    x = x_ref[0]
    h = _rmsnorm(x, nw_ref[...]).astype(bf16)
    proj = _dot(h, win_ref[...])
    u = proj[:, 0:e] * _sigmoid(proj[:, e:2 * e])
    off = HALO - (CONV_K - 1)
    cols = []
    for l in range(ng):
        ls = slice(l * LANES, (l + 1) * LANES)
        ubuf_ref[l, HALO:HALO + tb, :] = u[:, ls]
        a = dwb_ref[:, ls]
        for kk in range(CONV_K):
            a = a + ubuf_ref[l, off + kk:off + kk + tb, :] * dww_ref[kk:kk + 1, ls]
        cols.append(a)
        ubuf_ref[l, 0:HALO, :] = ubuf_ref[l, tb:tb + HALO, :]
    acc = jnp.concatenate(cols, axis=-1)

    mu = jnp.mean(acc, axis=-1, keepdims=True)
    xc = acc - mu
    var = jnp.mean(xc * xc, axis=-1, keepdims=True)
    ln = xc * lax.rsqrt(var + EPS) * lnw_ref[...] + lnb_ref[...]
    z = proj[:, 2 * e:3 * e]
    gated = ((ln * _sigmoid(ln)) * (z * _sigmoid(z))).astype(bf16)
    y = x + _dot(gated, wout_ref[...])
    if final_norm:
        y = _rmsnorm(y, fnw_ref[...])
    o_ref[0] = y


def _const_spec(shape):
    return pl.BlockSpec(shape, lambda b, t: (0,) * len(shape))


def _compiler_params():
    return pltpu.CompilerParams(dimension_semantics=("arbitrary", "arbitrary"),
                                vmem_limit_bytes=VMEM_LIMIT_BYTES)


def _gla_layer(x, norm_w, w_in, w_g2, b_g, gn_w, w_out):
    bsz, seq, d = x.shape
    tb = GLA_SUBBLOCKS * TIME_BLOCK
    x_spec = pl.BlockSpec((1, tb, d), lambda b, t: (b, t, 0))
    return pl.pallas_call(
        _gla_layer_kernel,
        grid=(bsz, seq // tb),
        in_specs=[x_spec, _const_spec(norm_w.shape), _const_spec(w_in.shape),
                  _const_spec(w_g2.shape), _const_spec(b_g.shape), _const_spec(gn_w.shape),
                  _const_spec(w_out.shape)],
        out_specs=x_spec,
        out_shape=jax.ShapeDtypeStruct(x.shape, x.dtype),
        scratch_shapes=[pltpu.VMEM((GLA_HEADS, HEAD_DV, HEAD_DK), jnp.float32),
                        pltpu.VMEM((tb, GLA_DV), jnp.bfloat16)],
        compiler_params=_compiler_params(),
        name="gla_layer",
    )(x, norm_w, w_in, w_g2, b_g, gn_w, w_out)


def _conv_layer(x, norm_w, w_in, dw_w, dw_b, ln_w, ln_b, w_out, final_w, final_norm):
    bsz, seq, d = x.shape
    tb = TIME_BLOCK
    x_spec = pl.BlockSpec((1, tb, d), lambda b, t: (b, t, 0))
    return pl.pallas_call(
        functools.partial(_conv_layer_kernel, final_norm=final_norm),
        grid=(bsz, seq // tb),
        in_specs=[x_spec, _const_spec(norm_w.shape), _const_spec(w_in.shape),
                  _const_spec(dw_w.shape), _const_spec(dw_b.shape), _const_spec(ln_w.shape),
                  _const_spec(ln_b.shape), _const_spec(w_out.shape), _const_spec(final_w.shape)],
        out_specs=x_spec,
        out_shape=jax.ShapeDtypeStruct(x.shape, x.dtype),
        scratch_shapes=[pltpu.VMEM((d // LANES, HALO + tb, LANES), jnp.float32)],
        compiler_params=_compiler_params(),
        name="conv_layer_final" if final_norm else "conv_layer",
    )(x, norm_w, w_in, dw_w, dw_b, ln_w, ln_b, w_out, final_w)


def kernel(x, norm_w, final_norm_w, gla_w_in, gla_w_g2, gla_b_g, gla_gn_w, gla_w_out,
           conv_w_in, conv_dw_w, conv_dw_b, conv_ln_w, conv_ln_b, conv_w_out):
    bf16 = jnp.bfloat16
    depth = norm_w.shape[0]
    assert depth % 2 == 0 and x.shape[1] % (GLA_SUBBLOCKS * TIME_BLOCK) == 0 and x.shape[2] == D_MODEL
    row = lambda a: a.reshape(1, -1)
    for i in range(depth):
        j = i // 2
        nw = row(norm_w[i])
        if i % 2 == 0:
            w_in = jnp.pad(gla_w_in[j], ((0, 0), (0, GATE_PAD - GATE_RANK))).astype(bf16)
            w_g2 = jnp.pad(gla_w_g2[j], ((0, GATE_PAD - GATE_RANK), (0, 0))).astype(bf16)
            x = _gla_layer(x, nw, w_in, w_g2, row(gla_b_g[j]), row(gla_gn_w[j]),
                           gla_w_out[j].astype(bf16))
        else:
            x = _conv_layer(x, nw, conv_w_in[j].astype(bf16), conv_dw_w[j], row(conv_dw_b[j]),
                            row(conv_ln_w[j]), row(conv_ln_b[j]), conv_w_out[j].astype(bf16),
                            row(final_norm_w), final_norm=(i == depth - 1))
    return x
```

```python
import functools

import jax
import jax.numpy as jnp
from jax import lax
from jax.experimental import pallas as pl
from jax.experimental.pallas import tpu as pltpu

EPS = 1e-6

D_MODEL = 1024
GLA_HEADS = 4
GLA_DK = D_MODEL // 2
GLA_DV = D_MODEL
HEAD_DK = GLA_DK // GLA_HEADS
HEAD_DV = GLA_DV // GLA_HEADS
GATE_RANK = 16
GATE_TAU = 16.0
CHUNK = 64
CONV_K = 31

LANES = 128
GATE_PAD = LANES
HALO = 32

TIME_BLOCK = 256
GLA_SUBBLOCKS = 4
CONV_SUBBLOCKS = 2
VMEM_LIMIT_BYTES = 56 * 1024 * 1024


NEG_LOG2_E = -1.4426950408889634


def _sigmoid(x):
    return 1.0 / (1.0 + jnp.exp2(x * NEG_LOG2_E))


def _rmsnorm(x, w):
    return x * lax.rsqrt(jnp.mean(x * x, axis=-1, keepdims=True) + EPS) * w


def _dot(a, b):
    return jnp.dot(a, b, preferred_element_type=jnp.float32)


def _dot_nt(a, b):
    return lax.dot_general(a, b, (((1,), (1,)), ((), ())),
                           preferred_element_type=jnp.float32)


def _dot_tn(a, b):
    return lax.dot_general(a, b, (((0,), (0,)), ((), ())),
                           preferred_element_type=jnp.float32)


def _gla_layer_kernel(x_ref, nw_ref, win_ref, wg2_ref, bg_ref, gnw_ref, wout_ref,
                      o_ref, st_ref, gated_ref):
    @pl.when(pl.program_id(1) == 0)
    def _():
        st_ref[...] = jnp.zeros_like(st_ref)

    tb = TIME_BLOCK
    for sb in range(x_ref.shape[1] // tb):
        rows = slice(sb * tb, (sb + 1) * tb)
        o_ref[0, rows, :] = _gla_block(x_ref[0, rows, :], nw_ref, win_ref, wg2_ref, bg_ref, gnw_ref,
                                       wout_ref, st_ref, gated_ref.at[rows])


def _gla_block(x, nw_ref, win_ref, wg2_ref, bg_ref, gnw_ref, wout_ref, st_ref, gated_ref):
    tb = x.shape[0]
    bf16 = jnp.bfloat16
    h = _rmsnorm(x, nw_ref[...]).astype(bf16)
    o_q, o_v, o_z, o_end = GATE_PAD, GATE_PAD + 2 * GLA_DK, GATE_PAD + 2 * GLA_DK + GLA_DV, win_ref.shape[1]
    proj = _dot(h, win_ref[...])
    g_lr = proj[:, 0:GATE_PAD]
    qk = proj[:, o_q:o_v]
    v_f32 = proj[:, o_v:o_z]
    z_all = proj[:, o_z:o_end]
    g = _dot(g_lr.astype(bf16), wg2_ref[...]) + bg_ref[...]
    log_a = (jnp.minimum(g, 0.0) - jnp.log(1.0 + jnp.exp(-jnp.abs(g)))) * (1.0 / GATE_TAU)

    ri = lax.broadcasted_iota(jnp.int32, (tb, tb), 0)
    ci = lax.broadcasted_iota(jnp.int32, (tb, tb), 1)
    in_chunk_causal = (ri // CHUNK == ci // CHUNK) & (ci <= ri)
    cum_mat = jnp.where(in_chunk_causal, 1.0, 0.0).astype(bf16)
    la_hi = log_a.astype(bf16)
    la_lo = (log_a - la_hi.astype(jnp.float32)).astype(bf16)
    b = _dot(cum_mat, la_hi) + _dot(cum_mat, la_lo)

    q = qk[:, 0:GLA_DK] * (HEAD_DK ** -0.5)
    k = qk[:, GLA_DK:2 * GLA_DK]
    nchunk = tb // CHUNK
    b_last = [b[(c + 1) * CHUNK - 1:(c + 1) * CHUNK, :] for c in range(nchunk)]
    b_end = jnp.concatenate([jnp.broadcast_to(bl, (CHUNK, GLA_DK)) for bl in b_last], axis=0)
    q_dec = (q * jnp.exp(b)).astype(bf16)
    k_inv = (k * jnp.exp(-b)).astype(bf16)
    k_end = (k * jnp.exp(b_end - b)).astype(bf16)
    decay = [jnp.exp(bl) for bl in b_last]
    v = v_f32.astype(bf16)
    zeros_k = jnp.zeros((CHUNK, HEAD_DK), bf16)

    for hd in range(GLA_HEADS):
        ks = slice(hd * HEAD_DK, (hd + 1) * HEAD_DK)
        vs = slice(hd * HEAD_DV, (hd + 1) * HEAD_DV)
        qd, v_h, ke = q_dec[:, ks], v[:, vs], k_end[:, ks]
        s = _dot_nt(qd, k_inv[:, ks])
        o = _dot(jnp.where(in_chunk_causal, s, 0.0).astype(bf16), v_h)
        incr = []
        for p in range(nchunk // 2):
            r0 = 2 * p * CHUNK
            ke_pair = jnp.concatenate(
                [jnp.concatenate([ke[r0:r0 + CHUNK], zeros_k], axis=1),
                 jnp.concatenate([zeros_k, ke[r0 + CHUNK:r0 + 2 * CHUNK]], axis=1)], axis=0)
            both = _dot_tn(v_h[r0:r0 + 2 * CHUNK], ke_pair)
            incr += [both[:, 0:HEAD_DK], both[:, HEAD_DK:2 * HEAD_DK]]
        st = st_ref[hd]
        inter = []
        for c in range(nchunk):
            inter.append(_dot_nt(qd[c * CHUNK:(c + 1) * CHUNK], st.astype(bf16)))
            st = st * decay[c][:, ks] + incr[c]
        st_ref[hd] = st
        o = o + jnp.concatenate(inter, axis=0)
        o = o * lax.rsqrt(jnp.mean(o * o, axis=-1, keepdims=True) + EPS) * gnw_ref[:, vs]
        z = z_all[:, vs]
        gated_ref[:, vs] = (o * (z * _sigmoid(z))).astype(bf16)

    return x + _dot(gated_ref[...], wout_ref[...])


def _conv_layer_kernel(x_ref, nw_ref, win_ref, dww_ref, dwb_ref, lnw_ref, lnb_ref, wout_ref,
                       fnw_ref, o_ref, ubuf_ref, *, final_norm):
    ng = D_MODEL // LANES
    step = x_ref.shape[1]
    tb = TIME_BLOCK

    @pl.when(pl.program_id(1) == 0)
    def _():
        ubuf_ref[:, 0:HALO, :] = jnp.zeros((ng, HALO, LANES), jnp.float32)

    for sb in range(step // tb):
        rows = slice(sb * tb, (sb + 1) * tb)
        y = _conv_block(x_ref[0, rows, :], sb * tb, nw_ref, win_ref, dww_ref, dwb_ref, lnw_ref,
                        lnb_ref, wout_ref, ubuf_ref)
        if final_norm:
            y = _rmsnorm(y, fnw_ref[...])
        o_ref[0, rows, :] = y
    ubuf_ref[:, 0:HALO, :] = ubuf_ref[:, step:step + HALO, :]


def _conv_block(x, row0, nw_ref, win_ref, dww_ref, dwb_ref, lnw_ref, lnb_ref, wout_ref, ubuf_ref):
    tb = x.shape[0]
    bf16 = jnp.bfloat16
    e = D_MODEL
    h = _rmsnorm(x, nw_ref[...]).astype(bf16)
    proj = _dot(h, win_ref[...])
    u = proj[:, 0:e] * _sigmoid(proj[:, e:2 * e])
    off = row0 + HALO - (CONV_K - 1)
    cols = []
    for l in range(e // LANES):
        ls = slice(l * LANES, (l + 1) * LANES)
        ubuf_ref[l, HALO + row0:HALO + row0 + tb, :] = u[:, ls]
        a = dwb_ref[:, ls]
        for kk in range(CONV_K):
            a = a + ubuf_ref[l, off + kk:off + kk + tb, :] * dww_ref[kk:kk + 1, ls]
        cols.append(a)
    acc = jnp.concatenate(cols, axis=-1)

    mu = jnp.mean(acc, axis=-1, keepdims=True)
    xc = acc - mu
    var = jnp.mean(xc * xc, axis=-1, keepdims=True)
    ln = xc * lax.rsqrt(var + EPS) * lnw_ref[...] + lnb_ref[...]
    z = proj[:, 2 * e:3 * e]
    gated = ((ln * _sigmoid(ln)) * (z * _sigmoid(z))).astype(bf16)
    return x + _dot(gated, wout_ref[...])


def _const_spec(shape):
    return pl.BlockSpec(shape, lambda b, t: (0,) * len(shape))


def _compiler_params():
    return pltpu.CompilerParams(dimension_semantics=("arbitrary", "arbitrary"),
                                vmem_limit_bytes=VMEM_LIMIT_BYTES)


def _gla_layer(x, norm_w, w_in, w_g2, b_g, gn_w, w_out):
    bsz, seq, d = x.shape
    tb = GLA_SUBBLOCKS * TIME_BLOCK
    x_spec = pl.BlockSpec((1, tb, d), lambda b, t: (b, t, 0))
    return pl.pallas_call(
        _gla_layer_kernel,
        grid=(bsz, seq // tb),
        in_specs=[x_spec, _const_spec(norm_w.shape), _const_spec(w_in.shape),
                  _const_spec(w_g2.shape), _const_spec(b_g.shape), _const_spec(gn_w.shape),
                  _const_spec(w_out.shape)],
        out_specs=x_spec,
        out_shape=jax.ShapeDtypeStruct(x.shape, x.dtype),
        scratch_shapes=[pltpu.VMEM((GLA_HEADS, HEAD_DV, HEAD_DK), jnp.float32),
                        pltpu.VMEM((tb, GLA_DV), jnp.bfloat16)],
        compiler_params=_compiler_params(),
        name="gla_layer",
    )(x, norm_w, w_in, w_g2, b_g, gn_w, w_out)


def _conv_layer(x, norm_w, w_in, dw_w, dw_b, ln_w, ln_b, w_out, final_w, final_norm):
    bsz, seq, d = x.shape
    tb = CONV_SUBBLOCKS * TIME_BLOCK
    x_spec = pl.BlockSpec((1, tb, d), lambda b, t: (b, t, 0))
    return pl.pallas_call(
        functools.partial(_conv_layer_kernel, final_norm=final_norm),
        grid=(bsz, seq // tb),
        in_specs=[x_spec, _const_spec(norm_w.shape), _const_spec(w_in.shape),
                  _const_spec(dw_w.shape), _const_spec(dw_b.shape), _const_spec(ln_w.shape),
                  _const_spec(ln_b.shape), _const_spec(w_out.shape), _const_spec(final_w.shape)],
        out_specs=x_spec,
        out_shape=jax.ShapeDtypeStruct(x.shape, x.dtype),
        scratch_shapes=[pltpu.VMEM((d // LANES, HALO + tb, LANES), jnp.float32)],
        compiler_params=_compiler_params(),
        name="conv_layer_final" if final_norm else "conv_layer",
    )(x, norm_w, w_in, dw_w, dw_b, ln_w, ln_b, w_out, final_w)


def kernel(x, norm_w, final_norm_w, gla_w_in, gla_w_g2, gla_b_g, gla_gn_w, gla_w_out,
           conv_w_in, conv_dw_w, conv_dw_b, conv_ln_w, conv_ln_b, conv_w_out):
    bf16 = jnp.bfloat16
    depth = norm_w.shape[0]
    assert depth % 2 == 0 and x.shape[1] % (GLA_SUBBLOCKS * TIME_BLOCK) == 0 and x.shape[2] == D_MODEL
    row = lambda a: a.reshape(1, -1)
    for i in range(depth):
        j = i // 2
        nw = row(norm_w[i])
        if i % 2 == 0:
            n_main = 2 * GLA_DK + 2 * GLA_DV
            w_gate = jnp.pad(gla_w_in[j][:, n_main:], ((0, 0), (0, GATE_PAD - GATE_RANK)))
            w_in = jnp.concatenate([w_gate, gla_w_in[j][:, :n_main]], axis=1).astype(bf16)
            w_g2 = jnp.pad(gla_w_g2[j], ((0, GATE_PAD - GATE_RANK), (0, 0))).astype(bf16)
            x = _gla_layer(x, nw, w_in, w_g2, row(gla_b_g[j]), row(gla_gn_w[j]),
                           gla_w_out[j].astype(bf16))
        else:
            x = _conv_layer(x, nw, conv_w_in[j].astype(bf16), conv_dw_w[j], row(conv_dw_b[j]),
                            row(conv_ln_w[j]), row(conv_ln_b[j]), conv_w_out[j].astype(bf16),
                            row(final_norm_w), final_norm=(i == depth - 1))
    return x
```

```python
import functools

import jax
import jax.numpy as jnp
import numpy as np
from jax import lax
from jax.experimental import pallas as pl
from jax.experimental.pallas import tpu as pltpu

EPS = 1e-6

D_MODEL = 1024
GLA_HEADS = 4
GLA_DK = D_MODEL // 2
GLA_DV = D_MODEL
HEAD_DK = GLA_DK // GLA_HEADS
HEAD_DV = GLA_DV // GLA_HEADS
GATE_RANK = 16
GATE_TAU = 16.0
CHUNK = 64
CONV_K = 31

LANES = 128
GATE_PAD = LANES
HALO = 32
DFT_OUT = 128
DFT_N = DFT_OUT + CONV_K - 1
DFT_NF = DFT_N // 2 + 1
DFT_WIN = DFT_OUT + HALO

TIME_BLOCK = 256
GLA_SUBBLOCKS = 4
CONV_SUBBLOCKS = 2
VMEM_LIMIT_BYTES = 56 * 1024 * 1024


NEG_LOG2_E = -1.4426950408889634


def _sigmoid(x):
    return 1.0 / (1.0 + jnp.exp2(x * NEG_LOG2_E))


def _rmsnorm(x, w):
    return x * lax.rsqrt(jnp.mean(x * x, axis=-1, keepdims=True) + EPS) * w


def _dot(a, b):
    return jnp.dot(a, b, preferred_element_type=jnp.float32)


def _dot_nt(a, b):
    return lax.dot_general(a, b, (((1,), (1,)), ((), ())),
                           preferred_element_type=jnp.float32)


def _dot_tn(a, b):
    return lax.dot_general(a, b, (((0,), (0,)), ((), ())),
                           preferred_element_type=jnp.float32)


def _gla_layer_kernel(x_ref, nw_ref, win_ref, wg2_ref, bg_ref, gnw_ref, wout_ref,
                      o_ref, st_ref, gated_ref):
    @pl.when(pl.program_id(1) == 0)
    def _():
        st_ref[...] = jnp.zeros_like(st_ref)

    tb = TIME_BLOCK
    for sb in range(x_ref.shape[1] // tb):
        rows = slice(sb * tb, (sb + 1) * tb)
        o_ref[0, rows, :] = _gla_block(x_ref[0, rows, :], nw_ref, win_ref, wg2_ref, bg_ref, gnw_ref,
                                       wout_ref, st_ref, gated_ref.at[rows])


def _gla_block(x, nw_ref, win_ref, wg2_ref, bg_ref, gnw_ref, wout_ref, st_ref, gated_ref):
    tb = x.shape[0]
    bf16 = jnp.bfloat16
    h = _rmsnorm(x, nw_ref[...]).astype(bf16)
    o_q, o_v, o_z, o_end = GATE_PAD, GATE_PAD + 2 * GLA_DK, GATE_PAD + 2 * GLA_DK + GLA_DV, win_ref.shape[1]
    proj = _dot(h, win_ref[...])
    g_lr = proj[:, 0:GATE_PAD]
    qk = proj[:, o_q:o_v]
    v_f32 = proj[:, o_v:o_z]
    z_all = proj[:, o_z:o_end]
    g = _dot(g_lr.astype(bf16), wg2_ref[...]) + bg_ref[...]
    log_a = (jnp.minimum(g, 0.0) - jnp.log(1.0 + jnp.exp(-jnp.abs(g)))) * (1.0 / GATE_TAU)

    ri = lax.broadcasted_iota(jnp.int32, (tb, tb), 0)
    ci = lax.broadcasted_iota(jnp.int32, (tb, tb), 1)
    in_chunk_causal = (ri // CHUNK == ci // CHUNK) & (ci <= ri)
    cum_mat = jnp.where(in_chunk_causal, 1.0, 0.0).astype(bf16)
    la_hi = log_a.astype(bf16)
    la_lo = (log_a - la_hi.astype(jnp.float32)).astype(bf16)
    b = _dot(cum_mat, la_hi) + _dot(cum_mat, la_lo)

    q = qk[:, 0:GLA_DK] * (HEAD_DK ** -0.5)
    k = qk[:, GLA_DK:2 * GLA_DK]
    nchunk = tb // CHUNK
    b_last = [b[(c + 1) * CHUNK - 1:(c + 1) * CHUNK, :] for c in range(nchunk)]
    b_end = jnp.concatenate([jnp.broadcast_to(bl, (CHUNK, GLA_DK)) for bl in b_last], axis=0)
    q_dec = (q * jnp.exp(b)).astype(bf16)
    k_inv = (k * jnp.exp(-b)).astype(bf16)
    k_end = (k * jnp.exp(b_end - b)).astype(bf16)
    decay = [jnp.exp(bl) for bl in b_last]
    v = v_f32.astype(bf16)
    zeros_k = jnp.zeros((CHUNK, HEAD_DK), bf16)

    for hd in range(GLA_HEADS):
        ks = slice(hd * HEAD_DK, (hd + 1) * HEAD_DK)
        vs = slice(hd * HEAD_DV, (hd + 1) * HEAD_DV)
        qd, v_h, ke = q_dec[:, ks], v[:, vs], k_end[:, ks]
        s = _dot_nt(qd, k_inv[:, ks])
        o = _dot(jnp.where(in_chunk_causal, s, 0.0).astype(bf16), v_h)
        incr = []
        for p in range(nchunk // 2):
            r0 = 2 * p * CHUNK
            ke_pair = jnp.concatenate(
                [jnp.concatenate([ke[r0:r0 + CHUNK], zeros_k], axis=1),
                 jnp.concatenate([zeros_k, ke[r0 + CHUNK:r0 + 2 * CHUNK]], axis=1)], axis=0)
            both = _dot_tn(v_h[r0:r0 + 2 * CHUNK], ke_pair)
            incr += [both[:, 0:HEAD_DK], both[:, HEAD_DK:2 * HEAD_DK]]
        st = st_ref[hd]
        inter = []
        for c in range(nchunk):
            inter.append(_dot_nt(qd[c * CHUNK:(c + 1) * CHUNK], st.astype(bf16)))
            st = st * decay[c][:, ks] + incr[c]
        st_ref[hd] = st
        o = o + jnp.concatenate(inter, axis=0)
        o = o * lax.rsqrt(jnp.mean(o * o, axis=-1, keepdims=True) + EPS) * gnw_ref[:, vs]
        z = z_all[:, vs]
        gated_ref[:, vs] = (o * (z * _sigmoid(z))).astype(bf16)

    return x + _dot(gated_ref[...], wout_ref[...])


def _dft_matrices():
    n = np.arange(DFT_N)
    f = np.arange(DFT_NF)
    theta = 2.0 * np.pi * np.outer(f, n) / DFT_N
    fwd = np.zeros((2 * DFT_NF, DFT_WIN))
    fwd[:DFT_NF, DFT_WIN - DFT_N:] = np.cos(theta)
    fwd[DFT_NF:, DFT_WIN - DFT_N:] = np.sin(theta)
    theta_k = 2.0 * np.pi * np.outer(f, CONV_K - 1 - np.arange(CONV_K)) / DFT_N
    tap_cos = np.zeros((DFT_NF, HALO))
    tap_sin = np.zeros((DFT_NF, HALO))
    tap_cos[:, :CONV_K] = np.cos(theta_k)
    tap_sin[:, :CONV_K] = np.sin(theta_k)
    theta_n = 2.0 * np.pi * np.outer(n[CONV_K - 1:], f) / DFT_N
    scale = np.full(DFT_NF, 2.0 / DFT_N)
    scale[0] = scale[DFT_NF - 1] = 1.0 / DFT_N
    inv = np.concatenate([np.cos(theta_n) * scale, np.sin(theta_n) * (2.0 / DFT_N)], axis=1)
    return (jnp.asarray(fwd, jnp.bfloat16), jnp.asarray(inv, jnp.bfloat16),
            jnp.asarray(tap_cos, jnp.float32), jnp.asarray(tap_sin, jnp.float32))


def _conv_layer_kernel(x_ref, nw_ref, win_ref, fwd_ref, inv_ref, tcos_ref, tsin_ref, dww_ref,
                       dwb_ref, lnw_ref, lnb_ref, wout_ref, fnw_ref, o_ref, ubuf_ref, spec_ref,
                       *, final_norm):
    step = x_ref.shape[1]
    tb = TIME_BLOCK

    @pl.when((pl.program_id(0) == 0) & (pl.program_id(1) == 0))
    def _():
        hi = lax.Precision.HIGHEST
        spec_ref[0] = jnp.dot(tcos_ref[...], dww_ref[...], precision=hi,
                              preferred_element_type=jnp.float32)
        spec_ref[1] = jnp.dot(tsin_ref[...], dww_ref[...], precision=hi,
                              preferred_element_type=jnp.float32)

    @pl.when(pl.program_id(1) == 0)
    def _():
        ubuf_ref[0:HALO, :] = jnp.zeros((HALO, D_MODEL), jnp.bfloat16)

    for sb in range(step // tb):
        rows = slice(sb * tb, (sb + 1) * tb)
        y = _conv_block(x_ref[0, rows, :], sb * tb, nw_ref, win_ref, fwd_ref, inv_ref, spec_ref,
                        dwb_ref, lnw_ref, lnb_ref, wout_ref, ubuf_ref)
        if final_norm:
            y = _rmsnorm(y, fnw_ref[...])
        o_ref[0, rows, :] = y
    ubuf_ref[0:HALO, :] = ubuf_ref[step:step + HALO, :]


def _conv_block(x, row0, nw_ref, win_ref, fwd_ref, inv_ref, spec_ref, dwb_ref, lnw_ref, lnb_ref,
                wout_ref, ubuf_ref):
    tb = x.shape[0]
    bf16 = jnp.bfloat16
    e = D_MODEL
    h = _rmsnorm(x, nw_ref[...]).astype(bf16)
    proj = _dot(h, win_ref[...])
    u = proj[:, 0:e] * _sigmoid(proj[:, e:2 * e])
    ubuf_ref[HALO + row0:HALO + row0 + tb, :] = u.astype(bf16)
    p, q = spec_ref[0], spec_ref[1]
    outs = []
    for w in range(tb // DFT_OUT):
        r0 = row0 + w * DFT_OUT
        spectrum = _dot(fwd_ref[...], ubuf_ref[r0:r0 + DFT_WIN, :])
        a, b = spectrum[0:DFT_NF], spectrum[DFT_NF:2 * DFT_NF]
        cd = jnp.concatenate([a * p - b * q, a * q + b * p], axis=0)
        outs.append(_dot(inv_ref[...], cd.astype(bf16)))
    acc = jnp.concatenate(outs, axis=0) + dwb_ref[...]

    mu = jnp.mean(acc, axis=-1, keepdims=True)
    xc = acc - mu
    var = jnp.mean(xc * xc, axis=-1, keepdims=True)
    ln = xc * lax.rsqrt(var + EPS) * lnw_ref[...] + lnb_ref[...]
    z = proj[:, 2 * e:3 * e]
    gated = ((ln * _sigmoid(ln)) * (z * _sigmoid(z))).astype(bf16)
    return x + _dot(gated, wout_ref[...])


def _const_spec(shape):
    return pl.BlockSpec(shape, lambda b, t: (0,) * len(shape))


def _compiler_params():
    return pltpu.CompilerParams(dimension_semantics=("arbitrary", "arbitrary"),
                                vmem_limit_bytes=VMEM_LIMIT_BYTES)


def _gla_layer(x, norm_w, w_in, w_g2, b_g, gn_w, w_out):
    bsz, seq, d = x.shape
    tb = GLA_SUBBLOCKS * TIME_BLOCK
    x_spec = pl.BlockSpec((1, tb, d), lambda b, t: (b, t, 0))
    return pl.pallas_call(
        _gla_layer_kernel,
        grid=(bsz, seq // tb),
        in_specs=[x_spec, _const_spec(norm_w.shape), _const_spec(w_in.shape),
                  _const_spec(w_g2.shape), _const_spec(b_g.shape), _const_spec(gn_w.shape),
                  _const_spec(w_out.shape)],
        out_specs=x_spec,
        out_shape=jax.ShapeDtypeStruct(x.shape, x.dtype),
        scratch_shapes=[pltpu.VMEM((GLA_HEADS, HEAD_DV, HEAD_DK), jnp.float32),
                        pltpu.VMEM((tb, GLA_DV), jnp.bfloat16)],
        compiler_params=_compiler_params(),
        name="gla_layer",
    )(x, norm_w, w_in, w_g2, b_g, gn_w, w_out)


def _conv_layer(x, norm_w, w_in, dw_w, dw_b, ln_w, ln_b, w_out, final_w, final_norm):
    bsz, seq, d = x.shape
    tb = CONV_SUBBLOCKS * TIME_BLOCK
    x_spec = pl.BlockSpec((1, tb, d), lambda b, t: (b, t, 0))
    operands = (norm_w, w_in) + _dft_matrices() + (dw_w, dw_b, ln_w, ln_b, w_out, final_w)
    return pl.pallas_call(
        functools.partial(_conv_layer_kernel, final_norm=final_norm),
        grid=(bsz, seq // tb),
        in_specs=[x_spec] + [_const_spec(a.shape) for a in operands],
        out_specs=x_spec,
        out_shape=jax.ShapeDtypeStruct(x.shape, x.dtype),
        scratch_shapes=[pltpu.VMEM((HALO + tb, d), jnp.bfloat16),
                        pltpu.VMEM((2, DFT_NF, d), jnp.float32)],
        compiler_params=_compiler_params(),
        name="conv_layer_final" if final_norm else "conv_layer",
    )(x, *operands)


def kernel(x, norm_w, final_norm_w, gla_w_in, gla_w_g2, gla_b_g, gla_gn_w, gla_w_out,
           conv_w_in, conv_dw_w, conv_dw_b, conv_ln_w, conv_ln_b, conv_w_out):
    bf16 = jnp.bfloat16
    depth = norm_w.shape[0]
    assert depth % 2 == 0 and x.shape[1] % (GLA_SUBBLOCKS * TIME_BLOCK) == 0 and x.shape[2] == D_MODEL
    row = lambda a: a.reshape(1, -1)
    for i in range(depth):
        j = i // 2
        nw = row(norm_w[i])
        if i % 2 == 0:
            n_main = 2 * GLA_DK + 2 * GLA_DV
            w_gate = jnp.pad(gla_w_in[j][:, n_main:], ((0, 0), (0, GATE_PAD - GATE_RANK)))
            w_in = jnp.concatenate([w_gate, gla_w_in[j][:, :n_main]], axis=1).astype(bf16)
            w_g2 = jnp.pad(gla_w_g2[j], ((0, GATE_PAD - GATE_RANK), (0, 0))).astype(bf16)
            x = _gla_layer(x, nw, w_in, w_g2, row(gla_b_g[j]), row(gla_gn_w[j]),
                           gla_w_out[j].astype(bf16))
        else:
            dw_w = jnp.pad(conv_dw_w[j], ((0, HALO - CONV_K), (0, 0)))
            x = _conv_layer(x, nw, conv_w_in[j].astype(bf16), dw_w, row(conv_dw_b[j]),
                            row(conv_ln_w[j]), row(conv_ln_b[j]), conv_w_out[j].astype(bf16),
                            row(final_norm_w), final_norm=(i == depth - 1))
    return x
```

```python
import functools

import jax
import jax.numpy as jnp
import numpy as np
from jax import lax
from jax.experimental import pallas as pl
from jax.experimental.pallas import tpu as pltpu

EPS = 1e-6

D_MODEL = 1024
GLA_HEADS = 4
GLA_DK = D_MODEL // 2
GLA_DV = D_MODEL
HEAD_DK = GLA_DK // GLA_HEADS
HEAD_DV = GLA_DV // GLA_HEADS
GATE_RANK = 16
GATE_TAU = 16.0
CHUNK = 64
CONV_K = 31

LANES = 128
GATE_PAD = LANES
HALO = 32
DFT_OUT = 128
DFT_N = DFT_OUT + CONV_K - 1
DFT_NF = DFT_N // 2 + 1
DFT_WIN = DFT_OUT + HALO

TIME_BLOCK = 256
GLA_SUBBLOCKS = 4
CONV_SUBBLOCKS = 4
VMEM_LIMIT_BYTES = 56 * 1024 * 1024


NEG_LOG2_E = -1.4426950408889634


def _sigmoid(x):
    return 1.0 / (1.0 + jnp.exp2(x * NEG_LOG2_E))


def _rmsnorm(x, w):
    return x * lax.rsqrt(jnp.mean(x * x, axis=-1, keepdims=True) + EPS) * w


def _dot(a, b):
    return jnp.dot(a, b, preferred_element_type=jnp.float32)


def _dot_nt(a, b):
    return lax.dot_general(a, b, (((1,), (1,)), ((), ())),
                           preferred_element_type=jnp.float32)


def _dot_tn(a, b):
    return lax.dot_general(a, b, (((0,), (0,)), ((), ())),
                           preferred_element_type=jnp.float32)


def _gla_layer_kernel(x_ref, nw_ref, win_ref, wg2_ref, bg_ref, gnw_ref, wout_ref,
                      o_ref, st_ref, gated_ref):
    @pl.when(pl.program_id(1) == 0)
    def _():
        st_ref[...] = jnp.zeros_like(st_ref)

    x = x_ref[0]
    h = _rmsnorm(x, nw_ref[...]).astype(jnp.bfloat16)
    tb = TIME_BLOCK
    blocks = [slice(sb * tb, (sb + 1) * tb) for sb in range(x_ref.shape[1] // tb)]
    projs = [_dot(h[rows], win_ref[...]) for rows in blocks]
    for rows, proj in zip(blocks, projs):
        _gla_block(proj, wg2_ref, bg_ref, gnw_ref, st_ref, gated_ref.at[rows])
    for rows in blocks:
        o_ref[0, rows, :] = x[rows] + _dot(gated_ref[rows, :], wout_ref[...])


def _gla_block(proj, wg2_ref, bg_ref, gnw_ref, st_ref, gated_ref):
    tb = proj.shape[0]
    bf16 = jnp.bfloat16
    o_q, o_v, o_z, o_end = GATE_PAD, GATE_PAD + 2 * GLA_DK, GATE_PAD + 2 * GLA_DK + GLA_DV, proj.shape[1]
    g_lr = proj[:, 0:GATE_PAD]
    qk = proj[:, o_q:o_v]
    v_f32 = proj[:, o_v:o_z]
    z_all = proj[:, o_z:o_end]
    g = _dot(g_lr.astype(bf16), wg2_ref[...]) + bg_ref[...]
    log_a = (jnp.minimum(g, 0.0) - jnp.log(1.0 + jnp.exp(-jnp.abs(g)))) * (1.0 / GATE_TAU)

    ri = lax.broadcasted_iota(jnp.int32, (tb, tb), 0)
    ci = lax.broadcasted_iota(jnp.int32, (tb, tb), 1)
    in_chunk_causal = (ri // CHUNK == ci // CHUNK) & (ci <= ri)
    cum_mat = jnp.where(in_chunk_causal, 1.0, 0.0).astype(bf16)
    b = _dot(cum_mat, log_a.astype(bf16))

    q = qk[:, 0:GLA_DK] * (HEAD_DK ** -0.5)
    k = qk[:, GLA_DK:2 * GLA_DK]
    nchunk = tb // CHUNK
    b_last = [b[(c + 1) * CHUNK - 1:(c + 1) * CHUNK, :] for c in range(nchunk)]
    b_end = jnp.concatenate([jnp.broadcast_to(bl, (CHUNK, GLA_DK)) for bl in b_last], axis=0)
    q_dec = (q * jnp.exp(b)).astype(bf16)
    k_inv = (k * jnp.exp(-b)).astype(bf16)
    k_end = (k * jnp.exp(b_end - b)).astype(bf16)
    decay = [jnp.exp(bl) for bl in b_last]
    v = v_f32.astype(bf16)
    zeros_k = jnp.zeros((CHUNK, HEAD_DK), bf16)

    for hd in range(GLA_HEADS):
        ks = slice(hd * HEAD_DK, (hd + 1) * HEAD_DK)
        vs = slice(hd * HEAD_DV, (hd + 1) * HEAD_DV)
        qd, v_h, ke = q_dec[:, ks], v[:, vs], k_end[:, ks]
        s = _dot_nt(qd, k_inv[:, ks])
        o = _dot(jnp.where(in_chunk_causal, s, 0.0).astype(bf16), v_h)
        incr = []
        for p in range(nchunk // 2):
            r0 = 2 * p * CHUNK
            ke_pair = jnp.concatenate(
                [jnp.concatenate([ke[r0:r0 + CHUNK], zeros_k], axis=1),
                 jnp.concatenate([zeros_k, ke[r0 + CHUNK:r0 + 2 * CHUNK]], axis=1)], axis=0)
            both = _dot_tn(v_h[r0:r0 + 2 * CHUNK], ke_pair)
            incr += [both[:, 0:HEAD_DK], both[:, HEAD_DK:2 * HEAD_DK]]
        st = st_ref[hd]
        inter = []
        for c in range(nchunk):
            inter.append(_dot_nt(qd[c * CHUNK:(c + 1) * CHUNK], st.astype(bf16)))
            st = st * decay[c][:, ks] + incr[c]
        st_ref[hd] = st
        o = o + jnp.concatenate(inter, axis=0)
        o = o * lax.rsqrt(jnp.mean(o * o, axis=-1, keepdims=True) + EPS) * gnw_ref[:, vs]
        z = z_all[:, vs]
        gated_ref[:, vs] = (o * (z * _sigmoid(z))).astype(bf16)


def _dft_matrices():
    n = np.arange(DFT_N)
    f = np.arange(DFT_NF)
    theta = 2.0 * np.pi * np.outer(f, n) / DFT_N
    fwd = np.zeros((2 * DFT_NF, DFT_WIN))
    fwd[:DFT_NF, DFT_WIN - DFT_N:] = np.cos(theta)
    fwd[DFT_NF:, DFT_WIN - DFT_N:] = np.sin(theta)
    theta_k = 2.0 * np.pi * np.outer(f, CONV_K - 1 - np.arange(CONV_K)) / DFT_N
    tap_cos = np.zeros((DFT_NF, HALO))
    tap_sin = np.zeros((DFT_NF, HALO))
    tap_cos[:, :CONV_K] = np.cos(theta_k)
    tap_sin[:, :CONV_K] = np.sin(theta_k)
    theta_n = 2.0 * np.pi * np.outer(n[CONV_K - 1:], f) / DFT_N
    scale = np.full(DFT_NF, 2.0 / DFT_N)
    scale[0] = scale[DFT_NF - 1] = 1.0 / DFT_N
    inv = np.concatenate([np.cos(theta_n) * scale, np.sin(theta_n) * (2.0 / DFT_N)], axis=1)
    return (jnp.asarray(fwd, jnp.bfloat16), jnp.asarray(inv, jnp.bfloat16),
            jnp.asarray(tap_cos, jnp.float32), jnp.asarray(tap_sin, jnp.float32))


def _conv_layer_kernel(x_ref, nw_ref, win_ref, fwd_ref, inv_ref, tcos_ref, tsin_ref, dww_ref,
                       dwb_ref, lnw_ref, lnb_ref, wout_ref, fnw_ref, o_ref, ubuf_ref, spec_ref,
                       *, final_norm):
    step = x_ref.shape[1]
    tb = TIME_BLOCK

    @pl.when((pl.program_id(0) == 0) & (pl.program_id(1) == 0))
    def _():
        hi = lax.Precision.HIGHEST
        spec_ref[0] = jnp.dot(tcos_ref[...], dww_ref[...], precision=hi,
                              preferred_element_type=jnp.float32)
        spec_ref[1] = jnp.dot(tsin_ref[...], dww_ref[...], precision=hi,
                              preferred_element_type=jnp.float32)

    @pl.when(pl.program_id(1) == 0)
    def _():
        ubuf_ref[0:HALO, :] = jnp.zeros((HALO, D_MODEL), jnp.bfloat16)

    x = x_ref[0]
    h = _rmsnorm(x, nw_ref[...]).astype(jnp.bfloat16)
    blocks = [slice(sb * tb, (sb + 1) * tb) for sb in range(step // tb)]
    projs = [_dot(h[rows], win_ref[...]) for rows in blocks]
    gated = [_conv_block(proj, rows.start, fwd_ref, inv_ref, spec_ref, dwb_ref, lnw_ref, lnb_ref,
                         ubuf_ref) for rows, proj in zip(blocks, projs)]
    for rows, g in zip(blocks, gated):
        y = x[rows] + _dot(g, wout_ref[...])
        if final_norm:
            y = _rmsnorm(y, fnw_ref[...])
        o_ref[0, rows, :] = y
    ubuf_ref[0:HALO, :] = ubuf_ref[step:step + HALO, :]


def _conv_block(proj, row0, fwd_ref, inv_ref, spec_ref, dwb_ref, lnw_ref, lnb_ref, ubuf_ref):
    tb = proj.shape[0]
    bf16 = jnp.bfloat16
    e = D_MODEL
    u = proj[:, 0:e] * _sigmoid(proj[:, e:2 * e])
    ubuf_ref[HALO + row0:HALO + row0 + tb, :] = u.astype(bf16)
    p, q = spec_ref[0], spec_ref[1]
    outs = []
    for w in range(tb // DFT_OUT):
        r0 = row0 + w * DFT_OUT
        spectrum = _dot(fwd_ref[...], ubuf_ref[r0:r0 + DFT_WIN, :])
        a, b = spectrum[0:DFT_NF], spectrum[DFT_NF:2 * DFT_NF]
        cd = jnp.concatenate([a * p - b * q, a * q + b * p], axis=0)
        outs.append(_dot(inv_ref[...], cd.astype(bf16)))
    acc = jnp.concatenate(outs, axis=0) + dwb_ref[...]

    mu = jnp.mean(acc, axis=-1, keepdims=True)
    xc = acc - mu
    var = jnp.mean(xc * xc, axis=-1, keepdims=True)
    ln = xc * lax.rsqrt(var + EPS) * lnw_ref[...] + lnb_ref[...]
    z = proj[:, 2 * e:3 * e]
    return ((ln * _sigmoid(ln)) * (z * _sigmoid(z))).astype(bf16)


def _const_spec(shape):
    return pl.BlockSpec(shape, lambda b, t: (0,) * len(shape))


def _compiler_params():
    return pltpu.CompilerParams(dimension_semantics=("arbitrary", "arbitrary"),
                                vmem_limit_bytes=VMEM_LIMIT_BYTES)


def _gla_layer(x, norm_w, w_in, w_g2, b_g, gn_w, w_out):
    bsz, seq, d = x.shape
    tb = GLA_SUBBLOCKS * TIME_BLOCK
    x_spec = pl.BlockSpec((1, tb, d), lambda b, t: (b, t, 0))
    return pl.pallas_call(
        _gla_layer_kernel,
        grid=(bsz, seq // tb),
        in_specs=[x_spec, _const_spec(norm_w.shape), _const_spec(w_in.shape),
                  _const_spec(w_g2.shape), _const_spec(b_g.shape), _const_spec(gn_w.shape),
                  _const_spec(w_out.shape)],
        out_specs=x_spec,
        out_shape=jax.ShapeDtypeStruct(x.shape, x.dtype),
        scratch_shapes=[pltpu.VMEM((GLA_HEADS, HEAD_DV, HEAD_DK), jnp.float32),
                        pltpu.VMEM((tb, GLA_DV), jnp.bfloat16)],
        compiler_params=_compiler_params(),
        name="gla_layer",
    )(x, norm_w, w_in, w_g2, b_g, gn_w, w_out)


def _conv_layer(x, norm_w, w_in, dw_w, dw_b, ln_w, ln_b, w_out, final_w, final_norm):
    bsz, seq, d = x.shape
    tb = CONV_SUBBLOCKS * TIME_BLOCK
    x_spec = pl.BlockSpec((1, tb, d), lambda b, t: (b, t, 0))
    operands = (norm_w, w_in) + _dft_matrices() + (dw_w, dw_b, ln_w, ln_b, w_out, final_w)
    return pl.pallas_call(
        functools.partial(_conv_layer_kernel, final_norm=final_norm),
        grid=(bsz, seq // tb),
        in_specs=[x_spec] + [_const_spec(a.shape) for a in operands],
        out_specs=x_spec,
        out_shape=jax.ShapeDtypeStruct(x.shape, x.dtype),
        scratch_shapes=[pltpu.VMEM((HALO + tb, d), jnp.bfloat16),
                        pltpu.VMEM((2, DFT_NF, d), jnp.float32)],
        compiler_params=_compiler_params(),
        name="conv_layer_final" if final_norm else "conv_layer",
    )(x, *operands)


def kernel(x, norm_w, final_norm_w, gla_w_in, gla_w_g2, gla_b_g, gla_gn_w, gla_w_out,
           conv_w_in, conv_dw_w, conv_dw_b, conv_ln_w, conv_ln_b, conv_w_out):
    bf16 = jnp.bfloat16
    depth = norm_w.shape[0]
    assert depth % 2 == 0 and x.shape[1] % (GLA_SUBBLOCKS * TIME_BLOCK) == 0 and x.shape[2] == D_MODEL
    row = lambda a: a.reshape(1, -1)
    for i in range(depth):
        j = i // 2
        nw = row(norm_w[i])
        if i % 2 == 0:
            n_main = 2 * GLA_DK + 2 * GLA_DV
            w_gate = jnp.pad(gla_w_in[j][:, n_main:], ((0, 0), (0, GATE_PAD - GATE_RANK)))
            w_in = jnp.concatenate([w_gate, gla_w_in[j][:, :n_main]], axis=1).astype(bf16)
            w_g2 = jnp.pad(gla_w_g2[j], ((0, GATE_PAD - GATE_RANK), (0, 0))).astype(bf16)
            x = _gla_layer(x, nw, w_in, w_g2, row(gla_b_g[j]), row(gla_gn_w[j]),
                           gla_w_out[j].astype(bf16))
        else:
            dw_w = jnp.pad(conv_dw_w[j], ((0, HALO - CONV_K), (0, 0)))
            x = _conv_layer(x, nw, conv_w_in[j].astype(bf16), dw_w, row(conv_dw_b[j]),
                            row(conv_ln_w[j]), row(conv_ln_b[j]), conv_w_out[j].astype(bf16),
                            row(final_norm_w), final_norm=(i == depth - 1))
    return x
```

```python
import functools

import jax
import jax.numpy as jnp
import numpy as np
from jax import lax
from jax.experimental import pallas as pl
from jax.experimental.pallas import tpu as pltpu

EPS = 1e-6

D_MODEL = 1024
GLA_HEADS = 4
GLA_DK = D_MODEL // 2
GLA_DV = D_MODEL
HEAD_DK = GLA_DK // GLA_HEADS
HEAD_DV = GLA_DV // GLA_HEADS
GATE_RANK = 16
GATE_TAU = 16.0
CHUNK = 64
CONV_K = 31

LANES = 128
GATE_PAD = LANES
HALO = 32
DFT_OUT = 128
DFT_N = DFT_OUT + CONV_K - 1
DFT_NF = DFT_N // 2 + 1
DFT_WIN = DFT_OUT + HALO

TIME_BLOCK = 256
GLA_SUBBLOCKS = 4
CONV_SUBBLOCKS = 4
VMEM_LIMIT_BYTES = 56 * 1024 * 1024


NEG_LOG2_E = -1.4426950408889634


def _sigmoid(x):
    return 1.0 / (1.0 + jnp.exp2(x * NEG_LOG2_E))


def _rmsnorm(x, w):
    return x * lax.rsqrt(jnp.mean(x * x, axis=-1, keepdims=True) + EPS) * w


def _dot(a, b):
    return jnp.dot(a, b, preferred_element_type=jnp.float32)


def _dot_nt(a, b):
    return lax.dot_general(a, b, (((1,), (1,)), ((), ())),
                           preferred_element_type=jnp.float32)


def _dot_tn(a, b):
    return lax.dot_general(a, b, (((0,), (0,)), ((), ())),
                           preferred_element_type=jnp.float32)


def _gla_layer_kernel(x_ref, nw_ref, win_ref, wg2_ref, bg_ref, gnw_ref, wout_ref,
                      o_ref, st_ref, gated_ref):
    @pl.when(pl.program_id(1) == 0)
    def _():
        st_ref[...] = jnp.zeros_like(st_ref)

    x = x_ref[0]
    h = _rmsnorm(x, nw_ref[...]).astype(jnp.bfloat16)
    tb = TIME_BLOCK
    blocks = [slice(sb * tb, (sb + 1) * tb) for sb in range(x_ref.shape[1] // tb)]
    projs = [_dot(h[rows], win_ref[...]) for rows in blocks]
    for rows, proj in zip(blocks, projs):
        _gla_block(proj, wg2_ref, bg_ref, gnw_ref, st_ref, gated_ref.at[rows])
    for rows in blocks:
        o_ref[0, rows, :] = x[rows] + _dot(gated_ref[rows, :], wout_ref[...])


def _gla_block(proj, wg2_ref, bg_ref, gnw_ref, st_ref, gated_ref):
    tb = proj.shape[0]
    bf16 = jnp.bfloat16
    o_q, o_v, o_z, o_end = GATE_PAD, GATE_PAD + 2 * GLA_DK, GATE_PAD + 2 * GLA_DK + GLA_DV, proj.shape[1]
    g_lr = proj[:, 0:GATE_PAD]
    qk = proj[:, o_q:o_v]
    v_f32 = proj[:, o_v:o_z]
    z_all = proj[:, o_z:o_end]
    g = _dot(g_lr.astype(bf16), wg2_ref[...]) + bg_ref[...]
    log_a = (jnp.minimum(g, 0.0) - jnp.log(1.0 + jnp.exp(-jnp.abs(g)))) * (1.0 / GATE_TAU)

    ri = lax.broadcasted_iota(jnp.int32, (tb, tb), 0)
    ci = lax.broadcasted_iota(jnp.int32, (tb, tb), 1)
    in_chunk_causal = (ri // CHUNK == ci // CHUNK) & (ci <= ri)
    cum_mat = jnp.where(in_chunk_causal, 1.0, 0.0).astype(bf16)
    b = _dot(cum_mat, log_a.astype(bf16))

    q = qk[:, 0:GLA_DK] * (HEAD_DK ** -0.5)
    k = qk[:, GLA_DK:2 * GLA_DK]
    nchunk = tb // CHUNK
    b_last = [b[(c + 1) * CHUNK - 1:(c + 1) * CHUNK, :] for c in range(nchunk)]
    b_end = jnp.concatenate([jnp.broadcast_to(bl, (CHUNK, GLA_DK)) for bl in b_last], axis=0)
    q_dec = (q * jnp.exp(b)).astype(bf16)
    k_inv = (k * jnp.exp(-b)).astype(bf16)
    k_end = (k * jnp.exp(b_end - b)).astype(bf16)
    decay = [jnp.exp(bl) for bl in b_last]
    v = v_f32.astype(bf16)
    zeros_k = jnp.zeros((CHUNK, HEAD_DK), bf16)

    for hd in range(GLA_HEADS):
        ks = slice(hd * HEAD_DK, (hd + 1) * HEAD_DK)
        vs = slice(hd * HEAD_DV, (hd + 1) * HEAD_DV)
        qd, v_h, ke = q_dec[:, ks], v[:, vs], k_end[:, ks]
        s = _dot_nt(qd, k_inv[:, ks])
        o = _dot(jnp.where(in_chunk_causal, s, 0.0).astype(bf16), v_h)
        incr = []
        for p in range(nchunk // 2):
            r0 = 2 * p * CHUNK
            ke_pair = jnp.concatenate(
                [jnp.concatenate([ke[r0:r0 + CHUNK], zeros_k], axis=1),
                 jnp.concatenate([zeros_k, ke[r0 + CHUNK:r0 + 2 * CHUNK]], axis=1)], axis=0)
            both = _dot_tn(v_h[r0:r0 + 2 * CHUNK], ke_pair)
            incr += [both[:, 0:HEAD_DK], both[:, HEAD_DK:2 * HEAD_DK]]
        st = st_ref[hd]
        inter = []
        for c in range(nchunk):
            inter.append(_dot_nt(qd[c * CHUNK:(c + 1) * CHUNK], st.astype(bf16)))
            st = st * decay[c][:, ks] + incr[c]
        st_ref[hd] = st
        o = o + jnp.concatenate(inter, axis=0)
        o = o * lax.rsqrt(jnp.mean(o * o, axis=-1, keepdims=True) + EPS) * gnw_ref[:, vs]
        z = z_all[:, vs]
        gated_ref[:, vs] = (o * (z * _sigmoid(z))).astype(bf16)


def _dft_matrices():
    n = np.arange(DFT_N)
    f = np.arange(DFT_NF)
    theta = 2.0 * np.pi * np.outer(f, n) / DFT_N
    fwd = np.zeros((2 * DFT_NF, DFT_WIN))
    fwd[:DFT_NF, DFT_WIN - DFT_N:] = np.cos(theta)
    fwd[DFT_NF:, DFT_WIN - DFT_N:] = np.sin(theta)
    theta_k = 2.0 * np.pi * np.outer(f, CONV_K - 1 - np.arange(CONV_K)) / DFT_N
    tap_cos = np.zeros((DFT_NF, HALO))
    tap_sin = np.zeros((DFT_NF, HALO))
    tap_cos[:, :CONV_K] = np.cos(theta_k)
    tap_sin[:, :CONV_K] = np.sin(theta_k)
    theta_n = 2.0 * np.pi * np.outer(n[CONV_K - 1:], f) / DFT_N
    scale = np.full(DFT_NF, 2.0 / DFT_N)
    scale[0] = scale[DFT_NF - 1] = 1.0 / DFT_N
    inv = np.concatenate([np.cos(theta_n) * scale, np.sin(theta_n) * (2.0 / DFT_N)], axis=1)
    return (jnp.asarray(fwd, jnp.bfloat16), jnp.asarray(inv, jnp.bfloat16),
            jnp.asarray(tap_cos, jnp.float32), jnp.asarray(tap_sin, jnp.float32))


def _conv_layer_kernel(x_ref, nw_ref, win_ref, fwd_ref, inv_ref, tcos_ref, tsin_ref, dww_ref,
                       dwb_ref, lnw_ref, lnb_ref, wout_ref, fnw_ref, o_ref, ubuf_ref, spec_ref,
                       *, final_norm):
    step = x_ref.shape[1]
    tb = TIME_BLOCK

    @pl.when((pl.program_id(0) == 0) & (pl.program_id(1) == 0))
    def _():
        hi = lax.Precision.HIGHEST
        spec_ref[0] = jnp.dot(tcos_ref[...], dww_ref[...], precision=hi,
                              preferred_element_type=jnp.float32)
        spec_ref[1] = jnp.dot(tsin_ref[...], dww_ref[...], precision=hi,
                              preferred_element_type=jnp.float32)

    @pl.when(pl.program_id(1) == 0)
    def _():
        ubuf_ref[...] = jnp.zeros_like(ubuf_ref)

    x = x_ref[0]
    h = _rmsnorm(x, nw_ref[...]).astype(jnp.bfloat16)
    blocks = [slice(sb * tb, (sb + 1) * tb) for sb in range(step // tb)]
    projs = [_dot(h[rows], win_ref[...]) for rows in blocks]
    gated, tail = [], ubuf_ref[...]
    for proj in projs:
        g, tail = _conv_block(proj, tail, fwd_ref, inv_ref, spec_ref, dwb_ref, lnw_ref, lnb_ref)
        gated.append(g)
    ubuf_ref[...] = tail
    for rows, g in zip(blocks, gated):
        y = x[rows] + _dot(g, wout_ref[...])
        if final_norm:
            y = _rmsnorm(y, fnw_ref[...])
        o_ref[0, rows, :] = y


def _conv_block(proj, tail, fwd_ref, inv_ref, spec_ref, dwb_ref, lnw_ref, lnb_ref):
    tb = proj.shape[0]
    bf16 = jnp.bfloat16
    e = D_MODEL
    u = jnp.concatenate([tail, (proj[:, 0:e] * _sigmoid(proj[:, e:2 * e])).astype(bf16)], axis=0)
    p, q = spec_ref[0], spec_ref[1]
    outs = []
    for w in range(tb // DFT_OUT):
        r0 = w * DFT_OUT
        spectrum = _dot(fwd_ref[...], u[r0:r0 + DFT_WIN])
        a, b = spectrum[0:DFT_NF], spectrum[DFT_NF:2 * DFT_NF]
        cd = jnp.concatenate([a * p - b * q, a * q + b * p], axis=0)
        outs.append(_dot(inv_ref[...], cd.astype(bf16)))
    acc = jnp.concatenate(outs, axis=0) + dwb_ref[...]

    mu = jnp.mean(acc, axis=-1, keepdims=True)
    xc = acc - mu
    var = jnp.mean(xc * xc, axis=-1, keepdims=True)
    ln = xc * lax.rsqrt(var + EPS) * lnw_ref[...] + lnb_ref[...]
    z = proj[:, 2 * e:3 * e]
    return ((ln * _sigmoid(ln)) * (z * _sigmoid(z))).astype(bf16), u[tb:tb + HALO]


def _const_spec(shape):
    return pl.BlockSpec(shape, lambda b, t: (0,) * len(shape))


def _layer_spec(stacked, layer):
    return pl.BlockSpec((None,) + stacked.shape[1:], lambda b, t: (layer, 0, 0))


def _compiler_params():
    return pltpu.CompilerParams(dimension_semantics=("arbitrary", "arbitrary"),
                                vmem_limit_bytes=VMEM_LIMIT_BYTES)


def _gla_layer(x, layer, norm_w, w_in, w_g2, b_g, gn_w, w_out):
    bsz, seq, d = x.shape
    tb = GLA_SUBBLOCKS * TIME_BLOCK
    x_spec = pl.BlockSpec((1, tb, d), lambda b, t: (b, t, 0))
    return pl.pallas_call(
        _gla_layer_kernel,
        grid=(bsz, seq // tb),
        in_specs=[x_spec, _const_spec(norm_w.shape), _layer_spec(w_in, layer),
                  _const_spec(w_g2.shape), _const_spec(b_g.shape), _const_spec(gn_w.shape),
                  _layer_spec(w_out, layer)],
        out_specs=x_spec,
        out_shape=jax.ShapeDtypeStruct(x.shape, x.dtype),
        scratch_shapes=[pltpu.VMEM((GLA_HEADS, HEAD_DV, HEAD_DK), jnp.float32),
                        pltpu.VMEM((tb, GLA_DV), jnp.bfloat16)],
        compiler_params=_compiler_params(),
        name="gla_layer",
    )(x, norm_w, w_in, w_g2, b_g, gn_w, w_out)


def _conv_layer(x, layer, norm_w, w_in, dw_w, dw_b, ln_w, ln_b, w_out, final_w, final_norm):
    bsz, seq, d = x.shape
    tb = CONV_SUBBLOCKS * TIME_BLOCK
    x_spec = pl.BlockSpec((1, tb, d), lambda b, t: (b, t, 0))
    operands = (norm_w, w_in) + _dft_matrices() + (dw_w, dw_b, ln_w, ln_b, w_out, final_w)
    return pl.pallas_call(
        functools.partial(_conv_layer_kernel, final_norm=final_norm),
        grid=(bsz, seq // tb),
        in_specs=[x_spec] + [_layer_spec(a, layer) if a.ndim == 3 else _const_spec(a.shape)
                             for a in operands],
        out_specs=x_spec,
        out_shape=jax.ShapeDtypeStruct(x.shape, x.dtype),
        scratch_shapes=[pltpu.VMEM((HALO, d), jnp.bfloat16),
                        pltpu.VMEM((2, DFT_NF, d), jnp.float32)],
        compiler_params=_compiler_params(),
        name="conv_layer_final" if final_norm else "conv_layer",
    )(x, *operands)


def kernel(x, norm_w, final_norm_w, gla_w_in, gla_w_g2, gla_b_g, gla_gn_w, gla_w_out,
           conv_w_in, conv_dw_w, conv_dw_b, conv_ln_w, conv_ln_b, conv_w_out):
    bf16 = jnp.bfloat16
    depth = norm_w.shape[0]
    assert depth % 2 == 0 and x.shape[1] % (GLA_SUBBLOCKS * TIME_BLOCK) == 0 and x.shape[2] == D_MODEL
    row = lambda a: a.reshape(1, -1)
    n_main = 2 * GLA_DK + 2 * GLA_DV
    gate_cols = jnp.pad(gla_w_in[:, :, n_main:], ((0, 0), (0, 0), (0, GATE_PAD - GATE_RANK)))
    gla_win = jnp.concatenate([gate_cols, gla_w_in[:, :, :n_main]], axis=2).astype(bf16)
    gla_wg2 = jnp.pad(gla_w_g2, ((0, 0), (0, GATE_PAD - GATE_RANK), (0, 0))).astype(bf16)
    gla_wout = gla_w_out.astype(bf16)
    conv_win = conv_w_in.astype(bf16)
    conv_wout = conv_w_out.astype(bf16)
    conv_dww = jnp.pad(conv_dw_w, ((0, 0), (0, HALO - CONV_K), (0, 0)))
    for i in range(depth):
        j = i // 2
        nw = row(norm_w[i])
        if i % 2 == 0:
            x = _gla_layer(x, j, nw, gla_win, gla_wg2[j], row(gla_b_g[j]), row(gla_gn_w[j]), gla_wout)
        else:
            x = _conv_layer(x, j, nw, conv_win, conv_dww[j], row(conv_dw_b[j]), row(conv_ln_w[j]),
                            row(conv_ln_b[j]), conv_wout, row(final_norm_w),
                            final_norm=(i == depth - 1))
    return x
```

```python
import functools

import jax
import jax.numpy as jnp
import numpy as np
from jax import lax
from jax.experimental import pallas as pl
from jax.experimental.pallas import tpu as pltpu

EPS = 1e-6

D_MODEL = 1024
GLA_HEADS = 4
GLA_DK = D_MODEL // 2
GLA_DV = D_MODEL
HEAD_DK = GLA_DK // GLA_HEADS
HEAD_DV = GLA_DV // GLA_HEADS
GATE_RANK = 16
GATE_TAU = 16.0
CHUNK = 64
CONV_K = 31

LANES = 128
GATE_PAD = LANES
HALO = 32
DFT_OUT = 128
DFT_N = DFT_OUT + CONV_K - 1
DFT_NF = DFT_N // 2 + 1
DFT_WIN = DFT_OUT + HALO

TIME_BLOCK = 256
GLA_SUBBLOCKS = 4
CONV_SUBBLOCKS = 4
VMEM_LIMIT_BYTES = 56 * 1024 * 1024


def _sigmoid(x):
    return 0.5 * jnp.tanh(0.5 * x) + 0.5


def _rmsnorm(x, w):
    return x * lax.rsqrt(jnp.mean(x * x, axis=-1, keepdims=True) + EPS) * w


def _dot(a, b):
    return jnp.dot(a, b, preferred_element_type=jnp.float32)


def _dot_nt(a, b):
    return lax.dot_general(a, b, (((1,), (1,)), ((), ())),
                           preferred_element_type=jnp.float32)


def _dot_tn(a, b):
    return lax.dot_general(a, b, (((0,), (0,)), ((), ())),
                           preferred_element_type=jnp.float32)


def _gla_layer_kernel(x_ref, nw_ref, win_ref, wg2_ref, bg_ref, gnw_ref, wout_ref,
                      o_ref, st_ref, gated_ref):
    @pl.when(pl.program_id(1) == 0)
    def _():
        st_ref[...] = jnp.zeros_like(st_ref)

    x = x_ref[0]
    h = _rmsnorm(x, nw_ref[...]).astype(jnp.bfloat16)
    tb = TIME_BLOCK
    blocks = [slice(sb * tb, (sb + 1) * tb) for sb in range(x_ref.shape[1] // tb)]
    projs = [_dot(h[rows], win_ref[...]) for rows in blocks]
    for rows, proj in zip(blocks, projs):
        _gla_block(proj, wg2_ref, bg_ref, gnw_ref, st_ref, gated_ref.at[rows])
    for rows in blocks:
        o_ref[0, rows, :] = x[rows] + _dot(gated_ref[rows, :], wout_ref[...])


def _gla_block(proj, wg2_ref, bg_ref, gnw_ref, st_ref, gated_ref):
    tb = proj.shape[0]
    bf16 = jnp.bfloat16
    o_q, o_v, o_z, o_end = GATE_PAD, GATE_PAD + 2 * GLA_DK, GATE_PAD + 2 * GLA_DK + GLA_DV, proj.shape[1]
    g_lr = proj[:, 0:GATE_PAD]
    qk = proj[:, o_q:o_v]
    v_f32 = proj[:, o_v:o_z]
    z_all = proj[:, o_z:o_end]
    g = _dot(g_lr.astype(bf16), wg2_ref[...]) + bg_ref[...]
    log_a = (jnp.minimum(g, 0.0) - jnp.log(1.0 + jnp.exp(-jnp.abs(g)))) * (1.0 / GATE_TAU)

    ri = lax.broadcasted_iota(jnp.int32, (tb, tb), 0)
    ci = lax.broadcasted_iota(jnp.int32, (tb, tb), 1)
    in_chunk_causal = (ri // CHUNK == ci // CHUNK) & (ci <= ri)
    cum_mat = jnp.where(in_chunk_causal, 1.0, 0.0).astype(bf16)
    b = _dot(cum_mat, log_a.astype(bf16))

    q = qk[:, 0:GLA_DK] * (HEAD_DK ** -0.5)
    k = qk[:, GLA_DK:2 * GLA_DK]
    nchunk = tb // CHUNK
    b_last = [b[(c + 1) * CHUNK - 1:(c + 1) * CHUNK, :] for c in range(nchunk)]
    b_end = jnp.concatenate([jnp.broadcast_to(bl, (CHUNK, GLA_DK)) for bl in b_last], axis=0)
    q_dec = (q * jnp.exp(b)).astype(bf16)
    k_inv = (k * jnp.exp(-b)).astype(bf16)
    k_end = (k * jnp.exp(b_end - b)).astype(bf16)
    decay = [jnp.exp(bl) for bl in b_last]
    v = v_f32.astype(bf16)
    zeros_k = jnp.zeros((CHUNK, HEAD_DK), bf16)

    for hd in range(GLA_HEADS):
        ks = slice(hd * HEAD_DK, (hd + 1) * HEAD_DK)
        vs = slice(hd * HEAD_DV, (hd + 1) * HEAD_DV)
        qd, v_h, ke = q_dec[:, ks], v[:, vs], k_end[:, ks]
        s = _dot_nt(qd, k_inv[:, ks])
        o = _dot(jnp.where(in_chunk_causal, s, 0.0).astype(bf16), v_h)
        incr = []
        for p in range(nchunk // 2):
            r0 = 2 * p * CHUNK
            ke_pair = jnp.concatenate(
                [jnp.concatenate([ke[r0:r0 + CHUNK], zeros_k], axis=1),
                 jnp.concatenate([zeros_k, ke[r0 + CHUNK:r0 + 2 * CHUNK]], axis=1)], axis=0)
            both = _dot_tn(v_h[r0:r0 + 2 * CHUNK], ke_pair)
            incr += [both[:, 0:HEAD_DK], both[:, HEAD_DK:2 * HEAD_DK]]
        st = st_ref[hd]
        inter = []
        for c in range(nchunk):
            inter.append(_dot_nt(qd[c * CHUNK:(c + 1) * CHUNK], st.astype(bf16)))
            st = st * decay[c][:, ks] + incr[c]
        st_ref[hd] = st
        o = o + jnp.concatenate(inter, axis=0)
        o = o * lax.rsqrt(jnp.mean(o * o, axis=-1, keepdims=True) + EPS) * gnw_ref[:, vs]
        z = z_all[:, vs]
        gated_ref[:, vs] = (o * (z * _sigmoid(z))).astype(bf16)


def _dft_matrices():
    n = np.arange(DFT_N)
    f = np.arange(DFT_NF)
    theta = 2.0 * np.pi * np.outer(f, n) / DFT_N
    fwd = np.zeros((2 * DFT_NF, DFT_WIN))
    fwd[:DFT_NF, DFT_WIN - DFT_N:] = np.cos(theta)
    fwd[DFT_NF:, DFT_WIN - DFT_N:] = np.sin(theta)
    theta_k = 2.0 * np.pi * np.outer(f, CONV_K - 1 - np.arange(CONV_K)) / DFT_N
    tap_cos = np.zeros((DFT_NF, HALO))
    tap_sin = np.zeros((DFT_NF, HALO))
    tap_cos[:, :CONV_K] = np.cos(theta_k)
    tap_sin[:, :CONV_K] = np.sin(theta_k)
    theta_n = 2.0 * np.pi * np.outer(n[CONV_K - 1:], f) / DFT_N
    scale = np.full(DFT_NF, 2.0 / DFT_N)
    scale[0] = scale[DFT_NF - 1] = 1.0 / DFT_N
    inv = np.concatenate([np.cos(theta_n) * scale, np.sin(theta_n) * (2.0 / DFT_N)], axis=1)
    return (jnp.asarray(fwd, jnp.bfloat16), jnp.asarray(inv, jnp.bfloat16),
            jnp.asarray(tap_cos, jnp.float32), jnp.asarray(tap_sin, jnp.float32))


def _conv_layer_kernel(x_ref, nw_ref, win_ref, fwd_ref, inv_ref, tcos_ref, tsin_ref, dww_ref,
                       dwb_ref, lnw_ref, lnb_ref, wout_ref, fnw_ref, o_ref, ubuf_ref, spec_ref,
                       *, final_norm):
    step = x_ref.shape[1]
    tb = TIME_BLOCK

    @pl.when((pl.program_id(0) == 0) & (pl.program_id(1) == 0))
    def _():
        hi = lax.Precision.HIGHEST
        spec_ref[0] = jnp.dot(tcos_ref[...], dww_ref[...], precision=hi,
                              preferred_element_type=jnp.float32)
        spec_ref[1] = jnp.dot(tsin_ref[...], dww_ref[...], precision=hi,
                              preferred_element_type=jnp.float32)

    @pl.when(pl.program_id(1) == 0)
    def _():
        ubuf_ref[...] = jnp.zeros_like(ubuf_ref)

    x = x_ref[0]
    h = _rmsnorm(x, nw_ref[...]).astype(jnp.bfloat16)
    blocks = [slice(sb * tb, (sb + 1) * tb) for sb in range(step // tb)]
    projs = [_dot(h[rows], win_ref[...]) for rows in blocks]
    gated, tail = [], ubuf_ref[...]
    for proj in projs:
        g, tail = _conv_block(proj, tail, fwd_ref, inv_ref, spec_ref, dwb_ref, lnw_ref, lnb_ref)
        gated.append(g)
    ubuf_ref[...] = tail
    for rows, g in zip(blocks, gated):
        y = x[rows] + _dot(g, wout_ref[...])
        if final_norm:
            y = _rmsnorm(y, fnw_ref[...])
        o_ref[0, rows, :] = y


def _conv_block(proj, tail, fwd_ref, inv_ref, spec_ref, dwb_ref, lnw_ref, lnb_ref):
    tb = proj.shape[0]
    bf16 = jnp.bfloat16
    e = D_MODEL
    u = jnp.concatenate([tail, (proj[:, 0:e] * _sigmoid(proj[:, e:2 * e])).astype(bf16)], axis=0)
    p, q = spec_ref[0], spec_ref[1]
    outs = []
    for w in range(tb // DFT_OUT):
        r0 = w * DFT_OUT
        spectrum = _dot(fwd_ref[...], u[r0:r0 + DFT_WIN])
        a, b = spectrum[0:DFT_NF], spectrum[DFT_NF:2 * DFT_NF]
        cd = jnp.concatenate([a * p - b * q, a * q + b * p], axis=0)
        outs.append(_dot(inv_ref[...], cd.astype(bf16)))
    acc = jnp.concatenate(outs, axis=0) + dwb_ref[...]

    mu = jnp.mean(acc, axis=-1, keepdims=True)
    xc = acc - mu
    var = jnp.mean(xc * xc, axis=-1, keepdims=True)
    ln = xc * lax.rsqrt(var + EPS) * lnw_ref[...] + lnb_ref[...]
    z = proj[:, 2 * e:3 * e]
    return ((ln * _sigmoid(ln)) * (z * _sigmoid(z))).astype(bf16), u[tb:tb + HALO]


def _const_spec(shape):
    return pl.BlockSpec(shape, lambda b, t: (0,) * len(shape))


def _layer_spec(stacked, layer):
    return pl.BlockSpec((None,) + stacked.shape[1:], lambda b, t: (layer, 0, 0))


def _compiler_params():
    return pltpu.CompilerParams(dimension_semantics=("arbitrary", "arbitrary"),
                                vmem_limit_bytes=VMEM_LIMIT_BYTES)


def _gla_layer(x, layer, norm_w, w_in, w_g2, b_g, gn_w, w_out):
    bsz, seq, d = x.shape
    tb = GLA_SUBBLOCKS * TIME_BLOCK
    x_spec = pl.BlockSpec((1, tb, d), lambda b, t: (b, t, 0))
    return pl.pallas_call(
        _gla_layer_kernel,
        grid=(bsz, seq // tb),
        in_specs=[x_spec, _const_spec(norm_w.shape), _layer_spec(w_in, layer),
                  _const_spec(w_g2.shape), _const_spec(b_g.shape), _const_spec(gn_w.shape),
                  _layer_spec(w_out, layer)],
        out_specs=x_spec,
        out_shape=jax.ShapeDtypeStruct(x.shape, x.dtype),
        scratch_shapes=[pltpu.VMEM((GLA_HEADS, HEAD_DV, HEAD_DK), jnp.float32),
                        pltpu.VMEM((tb, GLA_DV), jnp.bfloat16)],
        compiler_params=_compiler_params(),
        name="gla_layer",
    )(x, norm_w, w_in, w_g2, b_g, gn_w, w_out)


def _conv_layer(x, layer, norm_w, w_in, dw_w, dw_b, ln_w, ln_b, w_out, final_w, final_norm):
    bsz, seq, d = x.shape
    tb = CONV_SUBBLOCKS * TIME_BLOCK
    x_spec = pl.BlockSpec((1, tb, d), lambda b, t: (b, t, 0))
    operands = (norm_w, w_in) + _dft_matrices() + (dw_w, dw_b, ln_w, ln_b, w_out, final_w)
    return pl.pallas_call(
        functools.partial(_conv_layer_kernel, final_norm=final_norm),
        grid=(bsz, seq // tb),
        in_specs=[x_spec] + [_layer_spec(a, layer) if a.ndim == 3 else _const_spec(a.shape)
                             for a in operands],
        out_specs=x_spec,
        out_shape=jax.ShapeDtypeStruct(x.shape, x.dtype),
        scratch_shapes=[pltpu.VMEM((HALO, d), jnp.bfloat16),
                        pltpu.VMEM((2, DFT_NF, d), jnp.float32)],
        compiler_params=_compiler_params(),
        name="conv_layer_final" if final_norm else "conv_layer",
    )(x, *operands)


def kernel(x, norm_w, final_norm_w, gla_w_in, gla_w_g2, gla_b_g, gla_gn_w, gla_w_out,
           conv_w_in, conv_dw_w, conv_dw_b, conv_ln_w, conv_ln_b, conv_w_out):
    bf16 = jnp.bfloat16
    depth = norm_w.shape[0]
    assert depth % 2 == 0 and x.shape[1] % (GLA_SUBBLOCKS * TIME_BLOCK) == 0 and x.shape[2] == D_MODEL
    row = lambda a: a.reshape(1, -1)
    gla_win = jnp.pad(gla_w_in, ((0, 0), (0, 0), (0, GATE_PAD - GATE_RANK)))
    gla_win = jnp.roll(gla_win, GATE_PAD, axis=2).astype(bf16)
    gla_wg2 = jnp.pad(gla_w_g2, ((0, 0), (0, GATE_PAD - GATE_RANK), (0, 0))).astype(bf16)
    gla_wout = gla_w_out.astype(bf16)
    conv_win = conv_w_in.astype(bf16)
    conv_wout = conv_w_out.astype(bf16)
    conv_dww = jnp.pad(conv_dw_w, ((0, 0), (0, HALO - CONV_K), (0, 0)))
    for i in range(depth):
        j = i // 2
        nw = row(norm_w[i])
        if i % 2 == 0:
            x = _gla_layer(x, j, nw, gla_win, gla_wg2[j], row(gla_b_g[j]), row(gla_gn_w[j]), gla_wout)
        else:
            x = _conv_layer(x, j, nw, conv_win, conv_dww[j], row(conv_dw_b[j]), row(conv_ln_w[j]),
                            row(conv_ln_b[j]), conv_wout, row(final_norm_w),
                            final_norm=(i == depth - 1))
    return x
```

```python
import functools

import jax
import jax.numpy as jnp
import numpy as np
from jax import lax
from jax.experimental import pallas as pl
from jax.experimental.pallas import tpu as pltpu

EPS = 1e-6

D_MODEL = 1024
GLA_HEADS = 4
GLA_DK = D_MODEL // 2
GLA_DV = D_MODEL
HEAD_DK = GLA_DK // GLA_HEADS
HEAD_DV = GLA_DV // GLA_HEADS
GATE_RANK = 16
GATE_TAU = 16.0
CHUNK = 64
CONV_K = 31

LANES = 128
GATE_PAD = LANES
HALO = 32
DFT_OUT = 128
DFT_N = DFT_OUT + CONV_K - 1
DFT_NF = DFT_N // 2 + 1
DFT_WIN = DFT_OUT + HALO

TIME_BLOCK = 256
GLA_SUBBLOCKS = 4
CONV_SUBBLOCKS = 4
VMEM_LIMIT_BYTES = 56 * 1024 * 1024


def _sigmoid(x):
    return 0.5 * jnp.tanh(0.5 * x) + 0.5


def _rmsnorm(x, w):
    return x * lax.rsqrt(jnp.mean(x * x, axis=-1, keepdims=True) + EPS) * w


def _dot(a, b):
    return jnp.dot(a, b, preferred_element_type=jnp.float32)


def _dot_nt(a, b):
    return lax.dot_general(a, b, (((1,), (1,)), ((), ())),
                           preferred_element_type=jnp.float32)


def _dot_tn(a, b):
    return lax.dot_general(a, b, (((0,), (0,)), ((), ())),
                           preferred_element_type=jnp.float32)


def _gla_layer_kernel(x_ref, nw_ref, win_ref, wg2_ref, bg_ref, gnw_ref, wout_ref,
                      o_ref, st_ref, gated_ref):
    @pl.when(pl.program_id(1) == 0)
    def _():
        st_ref[...] = jnp.zeros_like(st_ref)

    x = x_ref[0]
    h = _rmsnorm(x, nw_ref[...]).astype(jnp.bfloat16)
    tb = TIME_BLOCK
    blocks = [slice(sb * tb, (sb + 1) * tb) for sb in range(x_ref.shape[1] // tb)]
    n_main = 2 * GLA_DK + 2 * GLA_DV
    w_in = jnp.concatenate([win_ref[:, n_main:], win_ref[:, 0:n_main]], axis=1)
    projs = [_dot(h[rows], w_in) for rows in blocks]
    for rows, proj in zip(blocks, projs):
        _gla_block(proj[:, 0:GATE_PAD], proj[:, GATE_PAD:], wg2_ref, bg_ref, gnw_ref, st_ref,
                   gated_ref.at[rows])
    for rows in blocks:
        o_ref[0, rows, :] = x[rows] + _dot(gated_ref[rows, :], wout_ref[...])


def _gla_block(g_lr, proj, wg2_ref, bg_ref, gnw_ref, st_ref, gated_ref):
    tb = proj.shape[0]
    bf16 = jnp.bfloat16
    o_q, o_v, o_z, o_end = 0, 2 * GLA_DK, 2 * GLA_DK + GLA_DV, proj.shape[1]
    qk = proj[:, o_q:o_v]
    v_f32 = proj[:, o_v:o_z]
    z_all = proj[:, o_z:o_end]
    g = _dot(g_lr.astype(bf16), wg2_ref[...]) + bg_ref[...]
    log_a = (jnp.minimum(g, 0.0) - jnp.log(1.0 + jnp.exp(-jnp.abs(g)))) * (1.0 / GATE_TAU)

    ri = lax.broadcasted_iota(jnp.int32, (tb, tb), 0)
    ci = lax.broadcasted_iota(jnp.int32, (tb, tb), 1)
    in_chunk_causal = (ri // CHUNK == ci // CHUNK) & (ci <= ri)
    cum_mat = jnp.where(in_chunk_causal, 1.0, 0.0).astype(bf16)
    b = _dot(cum_mat, log_a.astype(bf16))

    q = qk[:, 0:GLA_DK] * (HEAD_DK ** -0.5)
    k = qk[:, GLA_DK:2 * GLA_DK]
    nchunk = tb // CHUNK
    b_last = [b[(c + 1) * CHUNK - 1:(c + 1) * CHUNK, :] for c in range(nchunk)]
    b_end = jnp.concatenate([jnp.broadcast_to(bl, (CHUNK, GLA_DK)) for bl in b_last], axis=0)
    q_dec = (q * jnp.exp(b)).astype(bf16)
    k_inv = (k * jnp.exp(-b)).astype(bf16)
    k_end = (k * jnp.exp(b_end - b)).astype(bf16)
    decay = [jnp.exp(bl) for bl in b_last]
    v = v_f32.astype(bf16)
    zeros_k = jnp.zeros((CHUNK, HEAD_DK), bf16)

    for hd in range(GLA_HEADS):
        ks = slice(hd * HEAD_DK, (hd + 1) * HEAD_DK)
        vs = slice(hd * HEAD_DV, (hd + 1) * HEAD_DV)
        qd, v_h, ke = q_dec[:, ks], v[:, vs], k_end[:, ks]
        s = _dot_nt(qd, k_inv[:, ks])
        o = _dot(jnp.where(in_chunk_causal, s, 0.0).astype(bf16), v_h)
        incr = []
        for p in range(nchunk // 2):
            r0 = 2 * p * CHUNK
            ke_pair = jnp.concatenate(
                [jnp.concatenate([ke[r0:r0 + CHUNK], zeros_k], axis=1),
                 jnp.concatenate([zeros_k, ke[r0 + CHUNK:r0 + 2 * CHUNK]], axis=1)], axis=0)
            both = _dot_tn(v_h[r0:r0 + 2 * CHUNK], ke_pair)
            incr += [both[:, 0:HEAD_DK], both[:, HEAD_DK:2 * HEAD_DK]]
        st = st_ref[hd]
        inter = []
        for c in range(nchunk):
            inter.append(_dot_nt(qd[c * CHUNK:(c + 1) * CHUNK], st.astype(bf16)))
            st = st * decay[c][:, ks] + incr[c]
        st_ref[hd] = st
        o = o + jnp.concatenate(inter, axis=0)
        o = o * lax.rsqrt(jnp.mean(o * o, axis=-1, keepdims=True) + EPS) * gnw_ref[:, vs]
        z = z_all[:, vs]
        gated_ref[:, vs] = (o * (z * _sigmoid(z))).astype(bf16)


def _dft_matrices():
    n = np.arange(DFT_N)
    f = np.arange(DFT_NF)
    theta = 2.0 * np.pi * np.outer(f, n) / DFT_N
    fwd = np.zeros((2 * DFT_NF, DFT_WIN))
    fwd[:DFT_NF, DFT_WIN - DFT_N:] = np.cos(theta)
    fwd[DFT_NF:, DFT_WIN - DFT_N:] = np.sin(theta)
    theta_k = 2.0 * np.pi * np.outer(f, CONV_K - 1 - np.arange(CONV_K)) / DFT_N
    tap_cos = np.zeros((DFT_NF, HALO))
    tap_sin = np.zeros((DFT_NF, HALO))
    tap_cos[:, :CONV_K] = np.cos(theta_k)
    tap_sin[:, :CONV_K] = np.sin(theta_k)
    theta_n = 2.0 * np.pi * np.outer(n[CONV_K - 1:], f) / DFT_N
    scale = np.full(DFT_NF, 2.0 / DFT_N)
    scale[0] = scale[DFT_NF - 1] = 1.0 / DFT_N
    inv = np.concatenate([np.cos(theta_n) * scale, np.sin(theta_n) * (2.0 / DFT_N)], axis=1)
    return (jnp.asarray(fwd, jnp.bfloat16), jnp.asarray(inv, jnp.bfloat16),
            jnp.asarray(tap_cos, jnp.float32), jnp.asarray(tap_sin, jnp.float32))


def _conv_layer_kernel(x_ref, nw_ref, win_ref, fwd_ref, inv_ref, tcos_ref, tsin_ref, dww_ref,
                       dwb_ref, lnw_ref, lnb_ref, wout_ref, fnw_ref, o_ref, ubuf_ref, spec_ref,
                       *, final_norm):
    step = x_ref.shape[1]
    tb = TIME_BLOCK

    @pl.when((pl.program_id(0) == 0) & (pl.program_id(1) == 0))
    def _():
        hi = lax.Precision.HIGHEST
        spec_ref[0] = jnp.dot(tcos_ref[...], dww_ref[...], precision=hi,
                              preferred_element_type=jnp.float32)
        spec_ref[1] = jnp.dot(tsin_ref[...], dww_ref[...], precision=hi,
                              preferred_element_type=jnp.float32)

    @pl.when(pl.program_id(1) == 0)
    def _():
        ubuf_ref[...] = jnp.zeros_like(ubuf_ref)

    x = x_ref[0]
    h = _rmsnorm(x, nw_ref[...]).astype(jnp.bfloat16)
    blocks = [slice(sb * tb, (sb + 1) * tb) for sb in range(step // tb)]
    projs = [_dot(h[rows], win_ref[...]) for rows in blocks]
    gated, tail = [], ubuf_ref[...]
    for proj in projs:
        g, tail = _conv_block(proj, tail, fwd_ref, inv_ref, spec_ref, dwb_ref, lnw_ref, lnb_ref)
        gated.append(g)
    ubuf_ref[...] = tail
    for rows, g in zip(blocks, gated):
        y = x[rows] + _dot(g, wout_ref[...])
        if final_norm:
            y = _rmsnorm(y, fnw_ref[...])
        o_ref[0, rows, :] = y


def _conv_block(proj, tail, fwd_ref, inv_ref, spec_ref, dwb_ref, lnw_ref, lnb_ref):
    tb = proj.shape[0]
    bf16 = jnp.bfloat16
    e = D_MODEL
    u = jnp.concatenate([tail, (proj[:, 0:e] * _sigmoid(proj[:, e:2 * e])).astype(bf16)], axis=0)
    p, q = spec_ref[0], spec_ref[1]
    outs = []
    for w in range(tb // DFT_OUT):
        r0 = w * DFT_OUT
        spectrum = _dot(fwd_ref[...], u[r0:r0 + DFT_WIN])
        a, b = spectrum[0:DFT_NF], spectrum[DFT_NF:2 * DFT_NF]
        cd = jnp.concatenate([a * p - b * q, a * q + b * p], axis=0)
        outs.append(_dot(inv_ref[...], cd.astype(bf16)))
    acc = jnp.concatenate(outs, axis=0) + dwb_ref[...]

    mu = jnp.mean(acc, axis=-1, keepdims=True)
    xc = acc - mu
    var = jnp.mean(xc * xc, axis=-1, keepdims=True)
    ln = xc * lax.rsqrt(var + EPS) * lnw_ref[...] + lnb_ref[...]
    z = proj[:, 2 * e:3 * e]
    return ((ln * _sigmoid(ln)) * (z * _sigmoid(z))).astype(bf16), u[tb:tb + HALO]


def _const_spec(shape):
    return pl.BlockSpec(shape, lambda b, t: (0,) * len(shape))


def _layer_spec(stacked, layer):
    return pl.BlockSpec((None,) + stacked.shape[1:], lambda b, t: (layer, 0, 0))


def _compiler_params():
    return pltpu.CompilerParams(dimension_semantics=("arbitrary", "arbitrary"),
                                vmem_limit_bytes=VMEM_LIMIT_BYTES)


def _gla_layer(x, layer, norm_w, w_in, w_g2, b_g, gn_w, w_out):
    bsz, seq, d = x.shape
    tb = GLA_SUBBLOCKS * TIME_BLOCK
    x_spec = pl.BlockSpec((1, tb, d), lambda b, t: (b, t, 0))
    return pl.pallas_call(
        _gla_layer_kernel,
        grid=(bsz, seq // tb),
        in_specs=[x_spec, _const_spec(norm_w.shape), _layer_spec(w_in, layer),
                  _const_spec(w_g2.shape), _const_spec(b_g.shape), _const_spec(gn_w.shape),
                  _layer_spec(w_out, layer)],
        out_specs=x_spec,
        out_shape=jax.ShapeDtypeStruct(x.shape, x.dtype),
        scratch_shapes=[pltpu.VMEM((GLA_HEADS, HEAD_DV, HEAD_DK), jnp.float32),
                        pltpu.VMEM((tb, GLA_DV), jnp.bfloat16)],
        compiler_params=_compiler_params(),
        name="gla_layer",
    )(x, norm_w, w_in, w_g2, b_g, gn_w, w_out)


def _conv_layer(x, layer, norm_w, w_in, dw_w, dw_b, ln_w, ln_b, w_out, final_w, final_norm):
    bsz, seq, d = x.shape
    tb = CONV_SUBBLOCKS * TIME_BLOCK
    x_spec = pl.BlockSpec((1, tb, d), lambda b, t: (b, t, 0))
    operands = (norm_w, w_in) + _dft_matrices() + (dw_w, dw_b, ln_w, ln_b, w_out, final_w)
    return pl.pallas_call(
        functools.partial(_conv_layer_kernel, final_norm=final_norm),
        grid=(bsz, seq // tb),
        in_specs=[x_spec] + [_layer_spec(a, layer) if a.ndim == 3 else _const_spec(a.shape)
                             for a in operands],
        out_specs=x_spec,
        out_shape=jax.ShapeDtypeStruct(x.shape, x.dtype),
        scratch_shapes=[pltpu.VMEM((HALO, d), jnp.bfloat16),
                        pltpu.VMEM((2, DFT_NF, d), jnp.float32)],
        compiler_params=_compiler_params(),
        name="conv_layer_final" if final_norm else "conv_layer",
    )(x, *operands)


def kernel(x, norm_w, final_norm_w, gla_w_in, gla_w_g2, gla_b_g, gla_gn_w, gla_w_out,
           conv_w_in, conv_dw_w, conv_dw_b, conv_ln_w, conv_ln_b, conv_w_out):
    bf16 = jnp.bfloat16
    depth = norm_w.shape[0]
    assert depth % 2 == 0 and x.shape[1] % (GLA_SUBBLOCKS * TIME_BLOCK) == 0 and x.shape[2] == D_MODEL
    row = lambda a: a.reshape(1, -1)
    gla_win = jnp.pad(gla_w_in, ((0, 0), (0, 0), (0, GATE_PAD - GATE_RANK))).astype(bf16)
    gla_wg2 = jnp.pad(gla_w_g2, ((0, 0), (0, GATE_PAD - GATE_RANK), (0, 0))).astype(bf16)
    gla_wout = gla_w_out.astype(bf16)
    conv_win = conv_w_in.astype(bf16)
    conv_wout = conv_w_out.astype(bf16)
    conv_dww = jnp.pad(conv_dw_w, ((0, 0), (0, HALO - CONV_K), (0, 0)))
    for i in range(depth):
        j = i // 2
        nw = row(norm_w[i])
        if i % 2 == 0:
            x = _gla_layer(x, j, nw, gla_win, gla_wg2[j], row(gla_b_g[j]), row(gla_gn_w[j]), gla_wout)
        else:
            x = _conv_layer(x, j, nw, conv_win, conv_dww[j], row(conv_dw_b[j]), row(conv_ln_w[j]),
                            row(conv_ln_b[j]), conv_wout, row(final_norm_w),
                            final_norm=(i == depth - 1))
    return x
```

```python
import functools

import jax
import jax.numpy as jnp
import numpy as np
from jax import lax
from jax.experimental import pallas as pl
from jax.experimental.pallas import tpu as pltpu

EPS = 1e-6

D_MODEL = 1024
GLA_HEADS = 4
GLA_DK = D_MODEL // 2
GLA_DV = D_MODEL
HEAD_DK = GLA_DK // GLA_HEADS
HEAD_DV = GLA_DV // GLA_HEADS
GATE_RANK = 16
GATE_TAU = 16.0
CHUNK = 64
CONV_K = 31

LANES = 128
GATE_PAD = LANES
HALO = 32
DFT_OUT = 128
DFT_N = DFT_OUT + CONV_K - 1
DFT_NF = DFT_N // 2 + 1
DFT_WIN = DFT_OUT + HALO

TIME_BLOCK = 256
GLA_SUBBLOCKS = 4
CONV_SUBBLOCKS = 4
VMEM_LIMIT_BYTES = 56 * 1024 * 1024


def _sigmoid(x):
    return 0.5 * jnp.tanh(0.5 * x) + 0.5


def _rmsnorm(x, w):
    return x * lax.rsqrt(jnp.mean(x * x, axis=-1, keepdims=True) + EPS) * w


def _dot(a, b):
    return jnp.dot(a, b, preferred_element_type=jnp.float32)


def _dot_nt(a, b):
    return lax.dot_general(a, b, (((1,), (1,)), ((), ())),
                           preferred_element_type=jnp.float32)


def _dot_tn(a, b):
    return lax.dot_general(a, b, (((0,), (0,)), ((), ())),
                           preferred_element_type=jnp.float32)


def _gla_layer_kernel(x_ref, nw_ref, win_ref, wg2_ref, bg_ref, gnw_ref, wout_ref,
                      o_ref, st_ref, gated_ref):
    @pl.when(pl.program_id(1) == 0)
    def _():
        st_ref[...] = jnp.zeros_like(st_ref)

    x = x_ref[0]
    h = _rmsnorm(x, nw_ref[...]).astype(jnp.bfloat16)
    tb = TIME_BLOCK
    blocks = [slice(sb * tb, (sb + 1) * tb) for sb in range(x_ref.shape[1] // tb)]
    projs = [_dot(h[rows], win_ref[...]) for rows in blocks]
    for rows, proj in zip(blocks, projs):
        _gla_block(proj, wg2_ref, bg_ref, gnw_ref, st_ref, gated_ref.at[rows])
    for rows in blocks:
        o_ref[0, rows, :] = x[rows] + _dot(gated_ref[rows, :], wout_ref[...])


def _gla_block(proj, wg2_ref, bg_ref, gnw_ref, st_ref, gated_ref):
    tb = proj.shape[0]
    bf16 = jnp.bfloat16
    o_q, o_v, o_z, o_end = GATE_PAD, GATE_PAD + 2 * GLA_DK, GATE_PAD + 2 * GLA_DK + GLA_DV, proj.shape[1]
    g_lr = proj[:, 0:GATE_PAD]
    qk = proj[:, o_q:o_v]
    v_f32 = proj[:, o_v:o_z]
    z_all = proj[:, o_z:o_end]
    g = _dot(g_lr.astype(bf16), wg2_ref[...]) + bg_ref[...]
    log_a = (jnp.minimum(g, 0.0) - jnp.log(1.0 + jnp.exp(-jnp.abs(g)))) * (1.0 / GATE_TAU)

    ri = lax.broadcasted_iota(jnp.int32, (tb, tb), 0)
    ci = lax.broadcasted_iota(jnp.int32, (tb, tb), 1)
    in_chunk_causal = (ri // CHUNK == ci // CHUNK) & (ci <= ri)
    cum_mat = jnp.where(in_chunk_causal, 1.0, 0.0).astype(bf16)
    b = _dot(cum_mat, log_a.astype(bf16))

    q = qk[:, 0:GLA_DK] * (HEAD_DK ** -0.5)
    k = qk[:, GLA_DK:2 * GLA_DK]
    nchunk = tb // CHUNK
    b_last = [b[(c + 1) * CHUNK - 1:(c + 1) * CHUNK, :] for c in range(nchunk)]
    b_end = jnp.concatenate([jnp.broadcast_to(bl, (CHUNK, GLA_DK)) for bl in b_last], axis=0)
    q_dec = (q * jnp.exp(b)).astype(bf16)
    k_inv = (k * jnp.exp(-b)).astype(bf16)
    k_end = (k * jnp.exp(b_end - b)).astype(bf16)
    decay = [jnp.exp(bl) for bl in b_last]
    v = v_f32.astype(bf16)
    zeros_k = jnp.zeros((CHUNK, HEAD_DK), bf16)

    for hd in range(GLA_HEADS):
        ks = slice(hd * HEAD_DK, (hd + 1) * HEAD_DK)
        vs = slice(hd * HEAD_DV, (hd + 1) * HEAD_DV)
        qd, v_h, ke = q_dec[:, ks], v[:, vs], k_end[:, ks]
        s = _dot_nt(qd, k_inv[:, ks])
        o = _dot(jnp.where(in_chunk_causal, s, 0.0).astype(bf16), v_h)
        incr = []
        for p in range(nchunk // 2):
            r0 = 2 * p * CHUNK
            ke_pair = jnp.concatenate(
                [jnp.concatenate([ke[r0:r0 + CHUNK], zeros_k], axis=1),
                 jnp.concatenate([zeros_k, ke[r0 + CHUNK:r0 + 2 * CHUNK]], axis=1)], axis=0)
            both = _dot_tn(v_h[r0:r0 + 2 * CHUNK], ke_pair)
            incr += [both[:, 0:HEAD_DK], both[:, HEAD_DK:2 * HEAD_DK]]
        st = st_ref[hd]
        inter = []
        for c in range(nchunk):
            inter.append(_dot_nt(qd[c * CHUNK:(c + 1) * CHUNK], st.astype(bf16)))
            st = st * decay[c][:, ks] + incr[c]
        st_ref[hd] = st
        o = o + jnp.concatenate(inter, axis=0)
        o = o * lax.rsqrt(jnp.mean(o * o, axis=-1, keepdims=True) + EPS) * gnw_ref[:, vs]
        z = z_all[:, vs]
        gated_ref[:, vs] = (o * (z * _sigmoid(z))).astype(bf16)


def _dft_matrices():
    n = np.arange(DFT_N)
    f = np.arange(DFT_NF)
    theta = 2.0 * np.pi * np.outer(f, n) / DFT_N
    fwd = np.zeros((2 * DFT_NF, DFT_WIN))
    fwd[:DFT_NF, DFT_WIN - DFT_N:] = np.cos(theta)
    fwd[DFT_NF:, DFT_WIN - DFT_N:] = np.sin(theta)
    theta_k = 2.0 * np.pi * np.outer(f, CONV_K - 1 - np.arange(CONV_K)) / DFT_N
    tap_cos = np.zeros((DFT_NF, HALO))
    tap_sin = np.zeros((DFT_NF, HALO))
    tap_cos[:, :CONV_K] = np.cos(theta_k)
    tap_sin[:, :CONV_K] = np.sin(theta_k)
    theta_n = 2.0 * np.pi * np.outer(n[CONV_K - 1:], f) / DFT_N
    scale = np.full(DFT_NF, 2.0 / DFT_N)
    scale[0] = scale[DFT_NF - 1] = 1.0 / DFT_N
    inv = np.concatenate([np.cos(theta_n) * scale, np.sin(theta_n) * (2.0 / DFT_N)], axis=1)
    return (jnp.asarray(fwd, jnp.bfloat16), jnp.asarray(inv, jnp.bfloat16),
            jnp.asarray(tap_cos, jnp.float32), jnp.asarray(tap_sin, jnp.float32))


def _conv_layer_kernel(x_ref, nw_ref, win_ref, fwd_ref, inv_ref, tcos_ref, tsin_ref, dww_ref,
                       dwb_ref, lnw_ref, lnb_ref, wout_ref, fnw_ref, o_ref, ubuf_ref, spec_ref,
                       *, final_norm):
    step = x_ref.shape[1]
    tb = TIME_BLOCK

    @pl.when((pl.program_id(0) == 0) & (pl.program_id(1) == 0))
    def _():
        hi = lax.Precision.HIGHEST
        spec_ref[0] = jnp.dot(tcos_ref[...], dww_ref[...], precision=hi,
                              preferred_element_type=jnp.float32)
        spec_ref[1] = jnp.dot(tsin_ref[...], dww_ref[...], precision=hi,
                              preferred_element_type=jnp.float32)

    @pl.when(pl.program_id(1) == 0)
    def _():
        ubuf_ref[...] = jnp.zeros_like(ubuf_ref)

    x = x_ref[0]
    h = _rmsnorm(x, nw_ref[...]).astype(jnp.bfloat16)
    blocks = [slice(sb * tb, (sb + 1) * tb) for sb in range(step // tb)]
    projs = [_dot(h[rows], win_ref[...]) for rows in blocks]
    gated, tail = [], ubuf_ref[...]
    for proj in projs:
        g, tail = _conv_block(proj, tail, fwd_ref, inv_ref, spec_ref, dwb_ref, lnw_ref, lnb_ref)
        gated.append(g)
    ubuf_ref[...] = tail
    for rows, g in zip(blocks, gated):
        y = x[rows] + _dot(g, wout_ref[...])
        if final_norm:
            y = _rmsnorm(y, fnw_ref[...])
        o_ref[0, rows, :] = y


def _conv_block(proj, tail, fwd_ref, inv_ref, spec_ref, dwb_ref, lnw_ref, lnb_ref):
    tb = proj.shape[0]
    bf16 = jnp.bfloat16
    e = D_MODEL
    u = jnp.concatenate([tail, (proj[:, 0:e] * _sigmoid(proj[:, e:2 * e])).astype(bf16)], axis=0)
    p, q = spec_ref[0], spec_ref[1]
    outs = []
    for w in range(tb // DFT_OUT):
        r0 = w * DFT_OUT
        spectrum = _dot(fwd_ref[...], u[r0:r0 + DFT_WIN])
        a, b = spectrum[0:DFT_NF], spectrum[DFT_NF:2 * DFT_NF]
        cd = jnp.concatenate([a * p - b * q, a * q + b * p], axis=0)
        outs.append(_dot(inv_ref[...], cd.astype(bf16)))
    acc = jnp.concatenate(outs, axis=0) + dwb_ref[...]

    mu = jnp.mean(acc, axis=-1, keepdims=True)
    xc = acc - mu
    var = jnp.mean(xc * xc, axis=-1, keepdims=True)
    ln = xc * lax.rsqrt(var + EPS) * lnw_ref[...] + lnb_ref[...]
    z = proj[:, 2 * e:3 * e]
    return ((ln * _sigmoid(ln)) * (z * _sigmoid(z))).astype(bf16), u[tb:tb + HALO]


def _const_spec(shape):
    return pl.BlockSpec(shape, lambda b, t: (0,) * len(shape))


def _layer_spec(stacked, layer):
    return pl.BlockSpec((None,) + stacked.shape[1:], lambda b, t: (layer, 0, 0))


def _compiler_params():
    return pltpu.CompilerParams(dimension_semantics=("arbitrary", "arbitrary"),
                                vmem_limit_bytes=VMEM_LIMIT_BYTES)


def _gla_layer(x, layer, norm_w, w_in, w_g2, b_g, gn_w, w_out):
    bsz, seq, d = x.shape
    tb = GLA_SUBBLOCKS * TIME_BLOCK
    x_spec = pl.BlockSpec((1, tb, d), lambda b, t: (b, t, 0))
    return pl.pallas_call(
        _gla_layer_kernel,
        grid=(bsz, seq // tb),
        in_specs=[x_spec, _const_spec(norm_w.shape), _layer_spec(w_in, layer),
                  _const_spec(w_g2.shape), _const_spec(b_g.shape), _const_spec(gn_w.shape),
                  _layer_spec(w_out, layer)],
        out_specs=x_spec,
        out_shape=jax.ShapeDtypeStruct(x.shape, x.dtype),
        scratch_shapes=[pltpu.VMEM((GLA_HEADS, HEAD_DV, HEAD_DK), jnp.float32),
                        pltpu.VMEM((tb, GLA_DV), jnp.bfloat16)],
        compiler_params=_compiler_params(),
        name="gla_layer",
    )(x, norm_w, w_in, w_g2, b_g, gn_w, w_out)


def _conv_layer(x, layer, norm_w, w_in, dw_w, dw_b, ln_w, ln_b, w_out, final_w, final_norm):
    bsz, seq, d = x.shape
    tb = CONV_SUBBLOCKS * TIME_BLOCK
    x_spec = pl.BlockSpec((1, tb, d), lambda b, t: (b, t, 0))
    operands = (norm_w, w_in) + _dft_matrices() + (dw_w, dw_b, ln_w, ln_b, w_out, final_w)
    return pl.pallas_call(
        functools.partial(_conv_layer_kernel, final_norm=final_norm),
        grid=(bsz, seq // tb),
        in_specs=[x_spec] + [_layer_spec(a, layer) if a.ndim == 3 else _const_spec(a.shape)
                             for a in operands],
        out_specs=x_spec,
        out_shape=jax.ShapeDtypeStruct(x.shape, x.dtype),
        scratch_shapes=[pltpu.VMEM((HALO, d), jnp.bfloat16),
                        pltpu.VMEM((2, DFT_NF, d), jnp.float32)],
        compiler_params=_compiler_params(),
        name="conv_layer_final" if final_norm else "conv_layer",
    )(x, *operands)


def kernel(x, norm_w, final_norm_w, gla_w_in, gla_w_g2, gla_b_g, gla_gn_w, gla_w_out,
           conv_w_in, conv_dw_w, conv_dw_b, conv_ln_w, conv_ln_b, conv_w_out):
    bf16 = jnp.bfloat16
    depth = norm_w.shape[0]
    step = max(GLA_SUBBLOCKS, CONV_SUBBLOCKS) * TIME_BLOCK
    assert depth % 2 == 0 and x.shape[1] % step == 0 and x.shape[2] == D_MODEL
    row = lambda a: a.reshape(1, -1)
    gla_win = jnp.pad(gla_w_in, ((0, 0), (0, 0), (0, GATE_PAD - GATE_RANK)))
    gla_win = jnp.roll(gla_win, GATE_PAD, axis=2).astype(bf16)
    gla_wg2 = jnp.pad(gla_w_g2, ((0, 0), (0, GATE_PAD - GATE_RANK), (0, 0))).astype(bf16)
    gla_wout = gla_w_out.astype(bf16)
    conv_win = conv_w_in.astype(bf16)
    conv_wout = conv_w_out.astype(bf16)
    conv_dww = jnp.pad(conv_dw_w, ((0, 0), (0, HALO - CONV_K), (0, 0)))
    for i in range(depth):
        j = i // 2
        nw = row(norm_w[i])
        if i % 2 == 0:
            x = _gla_layer(x, j, nw, gla_win, gla_wg2[j], row(gla_b_g[j]), row(gla_gn_w[j]), gla_wout)
        else:
            x = _conv_layer(x, j, nw, conv_win, conv_dww[j], row(conv_dw_b[j]), row(conv_ln_w[j]),
                            row(conv_ln_b[j]), conv_wout, row(final_norm_w),
                            final_norm=(i == depth - 1))
    return x
```

```python
import functools

import jax
import jax.numpy as jnp
import numpy as np
from jax import lax
from jax.experimental import pallas as pl
from jax.experimental.pallas import tpu as pltpu

EPS = 1e-6

D_MODEL = 1024
GLA_HEADS = 4
GLA_DK = D_MODEL // 2
GLA_DV = D_MODEL
HEAD_DK = GLA_DK // GLA_HEADS
HEAD_DV = GLA_DV // GLA_HEADS
GATE_RANK = 16
GATE_TAU = 16.0
CHUNK = 64
CONV_K = 31

LANES = 128
GATE_PAD = LANES
HALO = 32
DFT_OUT = 128
DFT_N = DFT_OUT + CONV_K - 1
DFT_NF = DFT_N // 2 + 1
DFT_WIN = DFT_OUT + HALO

TIME_BLOCK = 256
GLA_SUBBLOCKS = 4
CONV_SUBBLOCKS = 4
VMEM_LIMIT_BYTES = 56 * 1024 * 1024


def _sigmoid(x):
    return 0.5 * jnp.tanh(0.5 * x) + 0.5


def _rmsnorm(x, w):
    return x * lax.rsqrt(jnp.mean(x * x, axis=-1, keepdims=True) + EPS) * w


def _dot(a, b):
    return jnp.dot(a, b, preferred_element_type=jnp.float32)


def _dot_nt(a, b):
    return lax.dot_general(a, b, (((1,), (1,)), ((), ())),
                           preferred_element_type=jnp.float32)


def _dot_tn(a, b):
    return lax.dot_general(a, b, (((0,), (0,)), ((), ())),
                           preferred_element_type=jnp.float32)


def _gla_layer_kernel(x_ref, nw_ref, win_ref, wg2_ref, bg_ref, gnw_ref, wout_ref,
                      o_ref, st_ref, gated_ref):
    @pl.when(pl.program_id(1) == 0)
    def _():
        st_ref[...] = jnp.zeros_like(st_ref)

    x = x_ref[0]
    h = _rmsnorm(x, nw_ref[...]).astype(jnp.bfloat16)
    tb = TIME_BLOCK
    blocks = [slice(sb * tb, (sb + 1) * tb) for sb in range(x_ref.shape[1] // tb)]
    projs = [_dot(h[rows], win_ref[...]) for rows in blocks]
    for rows, proj in zip(blocks, projs):
        _gla_block(proj, wg2_ref, bg_ref, gnw_ref, st_ref, gated_ref.at[rows])
    for rows in blocks:
        o_ref[0, rows, :] = x[rows] + _dot(gated_ref[rows, :], wout_ref[...])


def _gla_block(proj, wg2_ref, bg_ref, gnw_ref, st_ref, gated_ref):
    tb = proj.shape[0]
    bf16 = jnp.bfloat16
    o_q, o_v, o_z, o_end = GATE_PAD, GATE_PAD + 2 * GLA_DK, GATE_PAD + 2 * GLA_DK + GLA_DV, proj.shape[1]
    g_lr = proj[:, 0:GATE_PAD]
    qk = proj[:, o_q:o_v]
    v_f32 = proj[:, o_v:o_z]
    z_all = proj[:, o_z:o_end]
    g = _dot(g_lr.astype(bf16), wg2_ref[...]) + bg_ref[...]
    log_a = (jnp.minimum(g, 0.0) - jnp.log(1.0 + jnp.exp(-jnp.abs(g)))) * (1.0 / GATE_TAU)

    ri = lax.broadcasted_iota(jnp.int32, (tb, tb), 0)
    ci = lax.broadcasted_iota(jnp.int32, (tb, tb), 1)
    in_chunk_causal = (ri // CHUNK == ci // CHUNK) & (ci <= ri)
    cum_mat = jnp.where(in_chunk_causal, 1.0, 0.0).astype(bf16)
    b = _dot(cum_mat, log_a.astype(bf16))

    q = qk[:, 0:GLA_DK] * (HEAD_DK ** -0.5)
    k = qk[:, GLA_DK:2 * GLA_DK]
    nchunk = tb // CHUNK
    b_last = [b[(c + 1) * CHUNK - 1:(c + 1) * CHUNK, :] for c in range(nchunk)]
    b_end = jnp.concatenate([jnp.broadcast_to(bl, (CHUNK, GLA_DK)) for bl in b_last], axis=0)
    q_dec = (q * jnp.exp(b)).astype(bf16)
    k_inv = (k * jnp.exp(-b)).astype(bf16)
    k_end = (k * jnp.exp(b_end - b)).astype(bf16)
    decay = [jnp.exp(bl) for bl in b_last]
    v = v_f32.astype(bf16)
    zeros_k = jnp.zeros((CHUNK, HEAD_DK), bf16)

    for hd in range(GLA_HEADS):
        ks = slice(hd * HEAD_DK, (hd + 1) * HEAD_DK)
        vs = slice(hd * HEAD_DV, (hd + 1) * HEAD_DV)
        qd, v_h, ke = q_dec[:, ks], v[:, vs], k_end[:, ks]
        s = _dot_nt(qd, k_inv[:, ks])
        o = _dot(jnp.where(in_chunk_causal, s, 0.0).astype(bf16), v_h)
        incr = []
        for p in range(nchunk // 2):
            r0 = 2 * p * CHUNK
            ke_pair = jnp.concatenate(
                [jnp.concatenate([ke[r0:r0 + CHUNK], zeros_k], axis=1),
                 jnp.concatenate([zeros_k, ke[r0 + CHUNK:r0 + 2 * CHUNK]], axis=1)], axis=0)
            both = _dot_tn(v_h[r0:r0 + 2 * CHUNK], ke_pair)
            incr += [both[:, 0:HEAD_DK], both[:, HEAD_DK:2 * HEAD_DK]]
        st = st_ref[hd]
        inter = []
        for c in range(nchunk):
            inter.append(_dot_nt(qd[c * CHUNK:(c + 1) * CHUNK], st.astype(bf16)))
            st = st * decay[c][:, ks] + incr[c]
        st_ref[hd] = st
        o = o + jnp.concatenate(inter, axis=0)
        o = o * lax.rsqrt(jnp.mean(o * o, axis=-1, keepdims=True) + EPS) * gnw_ref[:, vs]
        z = z_all[:, vs]
        gated_ref[:, vs] = (o * (z * _sigmoid(z))).astype(bf16)


def _dft_matrices():
    n = np.arange(DFT_N)
    f = np.arange(DFT_NF)
    theta = 2.0 * np.pi * np.outer(f, n) / DFT_N
    fwd = np.zeros((2 * DFT_NF, DFT_WIN))
    fwd[:DFT_NF, DFT_WIN - DFT_N:] = np.cos(theta)
    fwd[DFT_NF:, DFT_WIN - DFT_N:] = np.sin(theta)
    theta_k = 2.0 * np.pi * np.outer(f, CONV_K - 1 - np.arange(CONV_K)) / DFT_N
    tap_cos = np.zeros((DFT_NF, HALO))
    tap_sin = np.zeros((DFT_NF, HALO))
    tap_cos[:, :CONV_K] = np.cos(theta_k)
    tap_sin[:, :CONV_K] = np.sin(theta_k)
    theta_n = 2.0 * np.pi * np.outer(n[CONV_K - 1:], f) / DFT_N
    scale = np.full(DFT_NF, 2.0 / DFT_N)
    scale[0] = scale[DFT_NF - 1] = 1.0 / DFT_N
    inv = np.concatenate([np.cos(theta_n) * scale, np.sin(theta_n) * (2.0 / DFT_N)], axis=1)
    return (jnp.asarray(fwd, jnp.bfloat16), jnp.asarray(inv, jnp.bfloat16),
            jnp.asarray(tap_cos, jnp.float32), jnp.asarray(tap_sin, jnp.float32))


def _conv_layer_kernel(x_ref, nw_ref, win_ref, fwd_ref, inv_ref, tcos_ref, tsin_ref, dww_ref,
                       dwb_ref, lnw_ref, lnb_ref, wout_ref, fnw_ref, o_ref, ubuf_ref, spec_ref,
                       *, final_norm):
    step = x_ref.shape[1]
    tb = TIME_BLOCK

    @pl.when((pl.program_id(0) == 0) & (pl.program_id(1) == 0))
    def _():
        hi = lax.Precision.HIGHEST
        spec_ref[0] = jnp.dot(tcos_ref[...], dww_ref[...], precision=hi,
                              preferred_element_type=jnp.float32)
        spec_ref[1] = jnp.dot(tsin_ref[...], dww_ref[...], precision=hi,
                              preferred_element_type=jnp.float32)

    @pl.when(pl.program_id(1) == 0)
    def _():
        ubuf_ref[...] = jnp.zeros_like(ubuf_ref)

    x = x_ref[0]
    h = _rmsnorm(x, nw_ref[...]).astype(jnp.bfloat16)
    blocks = [slice(sb * tb, (sb + 1) * tb) for sb in range(step // tb)]
    projs = [_dot(h[rows], win_ref[...]) for rows in blocks]
    gated, tail = [], ubuf_ref[...]
    for proj in projs:
        g, tail = _conv_block(proj, tail, fwd_ref, inv_ref, spec_ref, dwb_ref, lnw_ref, lnb_ref)
        gated.append(g)
    ubuf_ref[...] = tail
    for rows, g in zip(blocks, gated):
        y = x[rows] + _dot(g, wout_ref[...])
        if final_norm:
            y = _rmsnorm(y, fnw_ref[...])
        o_ref[0, rows, :] = y


def _conv_block(proj, tail, fwd_ref, inv_ref, spec_ref, dwb_ref, lnw_ref, lnb_ref):
    tb = proj.shape[0]
    bf16 = jnp.bfloat16
    e = D_MODEL
    u = jnp.concatenate([tail, (proj[:, 0:e] * _sigmoid(proj[:, e:2 * e])).astype(bf16)], axis=0)
    p, q = spec_ref[0], spec_ref[1]
    outs = []
    for w in range(tb // DFT_OUT):
        r0 = w * DFT_OUT
        spectrum = _dot(fwd_ref[...], u[r0:r0 + DFT_WIN])
        a, b = spectrum[0:DFT_NF], spectrum[DFT_NF:2 * DFT_NF]
        cd = jnp.concatenate([a * p - b * q, a * q + b * p], axis=0)
        outs.append(_dot(inv_ref[...], cd.astype(bf16)))
    acc = jnp.concatenate(outs, axis=0) + dwb_ref[...]

    mu = jnp.mean(acc, axis=-1, keepdims=True)
    xc = acc - mu
    var = jnp.mean(xc * xc, axis=-1, keepdims=True)
    ln = xc * lax.rsqrt(var + EPS) * lnw_ref[...] + lnb_ref[...]
    z = proj[:, 2 * e:3 * e]
    return ((ln * _sigmoid(ln)) * (z * _sigmoid(z))).astype(bf16), u[tb:tb + HALO]


def _const_spec(shape):
    return pl.BlockSpec(shape, lambda b, t: (0,) * len(shape))


def _layer_spec(stacked, layer):
    return pl.BlockSpec((None,) + stacked.shape[1:], lambda b, t: (layer, 0, 0))


def _compiler_params():
    return pltpu.CompilerParams(dimension_semantics=("arbitrary", "arbitrary"),
                                vmem_limit_bytes=VMEM_LIMIT_BYTES)


def _gla_layer(x, layer, norm_w, w_in, w_g2, b_g, gn_w, w_out):
    bsz, seq, d = x.shape
    tb = GLA_SUBBLOCKS * TIME_BLOCK
    x_spec = pl.BlockSpec((1, tb, d), lambda b, t: (b, t, 0))
    return pl.pallas_call(
        _gla_layer_kernel,
        grid=(bsz, seq // tb),
        in_specs=[x_spec, _const_spec(norm_w.shape), _layer_spec(w_in, layer),
                  _const_spec(w_g2.shape), _const_spec(b_g.shape), _const_spec(gn_w.shape),
                  _layer_spec(w_out, layer)],
        out_specs=x_spec,
        out_shape=jax.ShapeDtypeStruct(x.shape, x.dtype),
        scratch_shapes=[pltpu.VMEM((GLA_HEADS, HEAD_DV, HEAD_DK), jnp.float32),
                        pltpu.VMEM((tb, GLA_DV), jnp.bfloat16)],
        compiler_params=_compiler_params(),
        name="gla_layer",
    )(x, norm_w, w_in, w_g2, b_g, gn_w, w_out)


def _conv_layer(x, layer, norm_w, w_in, dw_w, dw_b, ln_w, ln_b, w_out, final_w, final_norm):
    bsz, seq, d = x.shape
    tb = CONV_SUBBLOCKS * TIME_BLOCK
    x_spec = pl.BlockSpec((1, tb, d), lambda b, t: (b, t, 0))
    operands = (norm_w, w_in) + _dft_matrices() + (dw_w, dw_b, ln_w, ln_b, w_out, final_w)
    return pl.pallas_call(
        functools.partial(_conv_layer_kernel, final_norm=final_norm),
        grid=(bsz, seq // tb),
        in_specs=[x_spec] + [_layer_spec(a, layer) if a.ndim == 3 else _const_spec(a.shape)
                             for a in operands],
        out_specs=x_spec,
        out_shape=jax.ShapeDtypeStruct(x.shape, x.dtype),
        scratch_shapes=[pltpu.VMEM((HALO, d), jnp.bfloat16),
                        pltpu.VMEM((2, DFT_NF, d), jnp.float32)],
        compiler_params=_compiler_params(),
        name="conv_layer_final" if final_norm else "conv_layer",
    )(x, *operands)


def kernel(x, norm_w, final_norm_w, gla_w_in, gla_w_g2, gla_b_g, gla_gn_w, gla_w_out,
           conv_w_in, conv_dw_w, conv_dw_b, conv_ln_w, conv_ln_b, conv_w_out):
    bf16 = jnp.bfloat16
    depth = norm_w.shape[0]
    step = max(GLA_SUBBLOCKS, CONV_SUBBLOCKS) * TIME_BLOCK
    assert depth % 2 == 0 and x.shape[1] % step == 0 and x.shape[2] == D_MODEL
    row = lambda a: a.reshape(1, -1)
    gla_win = jnp.pad(gla_w_in.astype(bf16), ((0, 0), (0, 0), (0, GATE_PAD - GATE_RANK)))
    gla_win = jnp.roll(gla_win, GATE_PAD, axis=2)
    gla_wg2 = jnp.pad(gla_w_g2, ((0, 0), (0, GATE_PAD - GATE_RANK), (0, 0))).astype(bf16)
    gla_wout = gla_w_out.astype(bf16)
    conv_win = conv_w_in.astype(bf16)
    conv_wout = conv_w_out.astype(bf16)
    conv_dww = jnp.pad(conv_dw_w, ((0, 0), (0, HALO - CONV_K), (0, 0)))
    for i in range(depth):
        j = i // 2
        nw = row(norm_w[i])
        if i % 2 == 0:
            x = _gla_layer(x, j, nw, gla_win, gla_wg2[j], row(gla_b_g[j]), row(gla_gn_w[j]), gla_wout)
        else:
            x = _conv_layer(x, j, nw, conv_win, conv_dww[j], row(conv_dw_b[j]), row(conv_ln_w[j]),
                            row(conv_ln_b[j]), conv_wout, row(final_norm_w),
                            final_norm=(i == depth - 1))
    return x
```
